```python
import math
import jax
import jax.numpy as jnp
from jax import lax
import numpy as np

D_MODEL = 1024
BATCH = 8
SEQ = 4096
DEPTH = 2
DEC_BATCH = 32
DEC_SEQ = 2048
PAST_LEN = 128

MEM_LEN = 256
N_MIXERS = 2
N_SSD_LAYERS = (DEPTH + N_MIXERS - 1) // N_MIXERS
N_DIFF_LAYERS = DEPTH // N_MIXERS
EPS = 1e-6

SSD_D_INNER = 2 * D_MODEL
SSD_HEAD_DIM = 64
SSD_N_HEADS = SSD_D_INNER // SSD_HEAD_DIM
SSD_N_GROUPS = 8
SSD_D_STATE = 128
SSD_BC = SSD_N_GROUPS * SSD_D_STATE
SSD_XBC = SSD_D_INNER + 2 * SSD_BC
SSD_CONV_WIDTH = 5
SSD_CONV_PAD = SSD_CONV_WIDTH // 2
SSD_CHUNK = 128
SSD_MIX_COLS = SSD_D_INNER + SSD_XBC + 2 * SSD_N_HEADS

DIFF_N_HEADS = 8
DIFF_HEAD_DIM = D_MODEL // (2 * DIFF_N_HEADS)
DIFF_V_DIM = 2 * DIFF_HEAD_DIM
DIFF_QK_COLS = DIFF_N_HEADS * 2 * DIFF_HEAD_DIM
DIFF_WIDTH = DIFF_N_HEADS * DIFF_V_DIM
DIFF_MIX_COLS = 2 * DIFF_QK_COLS + DIFF_WIDTH
Q_BLOCK = 128

X_N_HEADS = 4
X_HEAD_DIM = 256
X_WIDTH = X_N_HEADS * X_HEAD_DIM

REL_BUCKETS = 32
REL_MAX_DIST = 128

D_FF = 4 * D_MODEL

SSD_IN_COLS = SSD_MIX_COLS + X_WIDTH
DIFF_IN_COLS = DIFF_MIX_COLS + X_WIDTH

kernel_name = 'hybrid_ssd_diffattn_memory_encoder'


def rms(x):
    xf = x.astype(jnp.float32)
    return (xf * lax.rsqrt(jnp.mean(xf * xf, axis=-1, keepdims=True) + EPS)).astype(x.dtype)


def rmsnorm(x, g):
    return rms(x) * g


def t5_bucket(rel):
    nb = REL_BUCKETS // 2
    max_exact = nb // 2
    ret = jnp.where(rel > 0, nb, 0)
    n = jnp.abs(rel)
    nf = jnp.maximum(n, 1).astype(jnp.float32)
    large = max_exact + (jnp.log(nf / max_exact) / math.log(REL_MAX_DIST / max_exact)
                         * (nb - max_exact)).astype(jnp.int32)
    large = jnp.minimum(large, nb - 1)
    return ret + jnp.where(n < max_exact, n, large)


def rel_bias(table, q_pos, k_pos):
    bucket = t5_bucket(k_pos[None, :] - q_pos[:, None])
    return jnp.transpose(table[bucket].astype(jnp.float32), (2, 0, 1))


def ssd_chunked_scan(x, dt, a, bm, cm):
    b, l, h, p = x.shape
    g, n = bm.shape[2], bm.shape[3]
    r = h // g
    nc = l // SSD_CHUNK

    def chunks(t):
        return jnp.moveaxis(t.reshape((b, nc, SSD_CHUNK) + t.shape[2:]), 1, 0)

    xdt = chunks((x.astype(jnp.float32) * dt[..., None]).reshape(b, l, g, r, p))
    da = chunks((dt * a).reshape(b, l, g, r))
    bc = chunks(bm.astype(jnp.float32))
    cc = chunks(cm.astype(jnp.float32))
    lower = jnp.tril(jnp.ones((SSD_CHUNK, SSD_CHUNK), dtype=bool))

    def step(state, inp):
        xk, ak, bk, ck = inp
        cum = jnp.cumsum(ak, axis=1)
        seg = cum[:, :, None] - cum[:, None, :]
        decay = jnp.exp(jnp.where(lower[None, :, :, None, None], seg, -jnp.inf))
        w = jnp.einsum('btgn,bsgn->btsg', ck, bk)[..., None] * decay
        y = jnp.einsum('btsgr,bsgrp->btgrp', w, xk)
        y = y + jnp.einsum('btgn,bgrpn->btgrp', ck, state) * jnp.exp(cum)[..., None]
        to_end = jnp.exp(cum[:, -1:] - cum)
        state = (state * jnp.exp(cum[:, -1])[..., None, None]
                 + jnp.einsum('bsgn,bsgrp->bgrpn', bk, xk * to_end[..., None]))
        return state, y

    init = jnp.zeros((b, g, r, p, n), jnp.float32)
    _, ys = lax.scan(step, init, (xdt, da, bc, cc))
    return jnp.moveaxis(ys, 0, 1).reshape(b, l, h, p)


def _flip(t):
    return jnp.flip(t, axis=1)


def ssd_mixer(u, conv_w, conv_b, dt_bias, a_log, d_skip, norm_g):
    b, s, _ = u.shape
    z = u[..., :SSD_D_INNER]
    xbc = u[..., SSD_D_INNER:SSD_D_INNER + SSD_XBC]
    dt_raw = u[..., SSD_D_INNER + SSD_XBC:].reshape(b, s, 2, SSD_N_HEADS).astype(jnp.float32)
    xbc = lax.conv_general_dilated(
        xbc, conv_w[:, None, :], window_strides=(1,),
        padding=[(SSD_CONV_PAD, SSD_CONV_PAD)],
        dimension_numbers=('NWC', 'WIO', 'NWC'), feature_group_count=SSD_XBC)
    xbc = jax.nn.silu(xbc + conv_b)
    xs = xbc[..., :SSD_D_INNER].reshape(b, s, SSD_N_HEADS, SSD_HEAD_DIM)
    bm = xbc[..., SSD_D_INNER:SSD_D_INNER + SSD_BC].reshape(b, s, SSD_N_GROUPS, SSD_D_STATE)
    cm = xbc[..., SSD_D_INNER + SSD_BC:].reshape(b, s, SSD_N_GROUPS, SSD_D_STATE)
    dt = jax.nn.softplus(dt_raw + dt_bias.astype(jnp.float32))
    a = -jnp.exp(a_log.astype(jnp.float32))
    y_f = ssd_chunked_scan(xs, dt[:, :, 0], a[0], bm, cm)
    y_b = _flip(ssd_chunked_scan(_flip(xs), _flip(dt[:, :, 1]), a[1], _flip(bm), _flip(cm)))
    y = y_f + y_b + xs.astype(jnp.float32) * d_skip.astype(jnp.float32)[:, None]
    y = y.reshape(b, s, SSD_D_INNER).astype(u.dtype) * jax.nn.silu(z)
    y = rms(y.reshape(b, s, SSD_N_GROUPS, SSD_D_INNER // SSD_N_GROUPS)).reshape(b, s, SSD_D_INNER)
    return y * norm_g


def lambda_init(layer_idx):
    return 0.8 - 0.6 * math.exp(-0.3 * layer_idx)


def diff_attention(u, lam_params, subln_g, table, lam_init):
    b, s, _ = u.shape
    q = u[..., :DIFF_QK_COLS].reshape(b, s, DIFF_N_HEADS, 2, DIFF_HEAD_DIM)
    k = u[..., DIFF_QK_COLS:2 * DIFF_QK_COLS].reshape(b, s, DIFF_N_HEADS, 2, DIFF_HEAD_DIM)
    v = u[..., 2 * DIFF_QK_COLS:].reshape(b, s, DIFF_N_HEADS, DIFF_V_DIM)
    lp = lam_params.astype(jnp.float32)
    lam = jnp.exp(jnp.sum(lp[0] * lp[1])) - jnp.exp(jnp.sum(lp[2] * lp[3])) + lam_init
    k1, k2 = k[..., 0, :], k[..., 1, :]
    scale = DIFF_HEAD_DIM ** -0.5
    nblk = s // Q_BLOCK
    k_pos = jnp.arange(s, dtype=jnp.int32)

    def blocks(t):
        return jnp.moveaxis(t.reshape(b, nblk, Q_BLOCK, DIFF_N_HEADS, DIFF_HEAD_DIM), 1, 0)

    def attend(args):
        q1b, q2b, start = args
        q_pos = start + jnp.arange(Q_BLOCK, dtype=jnp.int32)
        bias = rel_bias(table, q_pos, k_pos)[None]
        s1 = jnp.einsum('bqhd,bkhd->bhqk', q1b, k1).astype(jnp.float32) * scale + bias
        s2 = jnp.einsum('bqhd,bkhd->bhqk', q2b, k2).astype(jnp.float32) * scale + bias
        pr = (jax.nn.softmax(s1, axis=-1) - lam * jax.nn.softmax(s2, axis=-1)).astype(v.dtype)
        return jnp.einsum('bhqk,bkhe->bqhe', pr, v)

    starts = jnp.arange(nblk, dtype=jnp.int32) * Q_BLOCK
    out = lax.map(attend, (blocks(q[..., 0, :]), blocks(q[..., 1, :]), starts))
    out = jnp.moveaxis(out, 0, 1).reshape(b, s, DIFF_N_HEADS, DIFF_V_DIM)
    out = rmsnorm(out, subln_g) * (1.0 - lam_init)
    return out.reshape(b, s, DIFF_WIDTH).astype(u.dtype)


def memory_cross_attention(q, mem, mem_g, w_kv):
    b, s, _ = q.shape
    kv = rmsnorm(mem, mem_g) @ w_kv
    km = kv[..., :X_WIDTH].reshape(b, MEM_LEN, X_N_HEADS, X_HEAD_DIM)
    vm = kv[..., X_WIDTH:].reshape(b, MEM_LEN, X_N_HEADS, X_HEAD_DIM)
    qh = q.reshape(b, s, X_N_HEADS, X_HEAD_DIM)
    logits = jnp.einsum('bshd,bmhd->bhsm', qh, km).astype(jnp.float32) * (X_HEAD_DIM ** -0.5)
    pr = jax.nn.softmax(logits, axis=-1).astype(vm.dtype)
    return jnp.einsum('bhsm,bmhd->bshd', pr, vm).reshape(b, s, X_WIDTH)


def trunk(x, mem, p):
    for i in range(DEPTH):
        j = i // N_MIXERS
        h = rmsnorm(x, p['norm_pre_mix'][i])
        if i % N_MIXERS == 0:
            u = h @ p['ssd_w_in'][j]
            mix = ssd_mixer(u[..., :SSD_MIX_COLS], p['ssd_conv_w'][j], p['ssd_conv_b'][j],
                            p['ssd_dt_bias'][j], p['ssd_a_log'][j], p['ssd_d'][j], p['ssd_norm'][j])
            w_out = p['ssd_w_out'][j]
        else:
            u = h @ p['diff_w_in'][j]
            mix = diff_attention(u[..., :DIFF_MIX_COLS], p['diff_lambda'][j], p['diff_subln'][j],
                                 p['rel_bias_table'], lambda_init(i))
            w_out = p['diff_w_out'][j]
        mem_out = memory_cross_attention(u[..., -X_WIDTH:], mem, p['x_mem_norm'][i], p['x_w_kv'][i])
        o = jnp.concatenate([mix.astype(x.dtype), mem_out.astype(x.dtype)], axis=-1) @ w_out
        x = x + rmsnorm(o, p['norm_post_mix'][i])
        h = rmsnorm(x, p['norm_pre_mlp'][i])
        f = jnp.square(jax.nn.relu(h @ p['mlp_w1'][i])) @ p['mlp_w2'][i]
        x = x + rmsnorm(f, p['norm_post_mlp'][i])
    return x


def setup_inputs(seed: int = 0) -> dict:
    key = jax.random.key(seed)
    ks = jax.random.split(key, 32)
    f32 = jnp.float32

    def nrm(k, shape, scale):
        return jax.random.normal(k, shape, f32) * scale

    def gain(k, shape):
        return 1.0 + 0.02 * jax.random.normal(k, shape, f32)

    dt0 = jnp.exp(jax.random.uniform(ks[10], (N_SSD_LAYERS, 2, SSD_N_HEADS), f32,
                                     math.log(1e-3), math.log(1e-1)))
    dt_bias = dt0 + jnp.log(-jnp.expm1(-dt0))
    a_log = jnp.log(jax.random.uniform(ks[11], (N_SSD_LAYERS, 2, SSD_N_HEADS), f32, 1.0, 16.0))
    return {
        'x_prompt': nrm(ks[0], (BATCH, SEQ, D_MODEL), 1.0),
        'x_sample': nrm(ks[1], (DEC_BATCH, DEC_SEQ, D_MODEL), 1.0),
        'mem_prompt': nrm(ks[2], (BATCH, MEM_LEN, D_MODEL), 1.0),
        'mem_sample': nrm(ks[3], (DEC_BATCH, MEM_LEN, D_MODEL), 1.0),
        'rel_bias_table': nrm(ks[4], (REL_BUCKETS, DIFF_N_HEADS), 0.5),
        'norm_pre_mix': gain(ks[5], (DEPTH, D_MODEL)),
        'norm_post_mix': gain(ks[6], (DEPTH, D_MODEL)),
        'norm_pre_mlp': gain(ks[7], (DEPTH, D_MODEL)),
        'norm_post_mlp': gain(ks[8], (DEPTH, D_MODEL)),
        'ssd_w_in': nrm(ks[9], (N_SSD_LAYERS, D_MODEL, SSD_IN_COLS), D_MODEL ** -0.5),
        'ssd_conv_w': nrm(ks[12], (N_SSD_LAYERS, SSD_CONV_WIDTH, SSD_XBC), SSD_CONV_WIDTH ** -0.5),
        'ssd_conv_b': nrm(ks[13], (N_SSD_LAYERS, SSD_XBC), 0.01),
        'ssd_dt_bias': dt_bias,
        'ssd_a_log': a_log,
        'ssd_d': gain(ks[14], (N_SSD_LAYERS, SSD_N_HEADS)),
        'ssd_norm': gain(ks[15], (N_SSD_LAYERS, SSD_D_INNER)),
        'ssd_w_out': nrm(ks[16], (N_SSD_LAYERS, SSD_D_INNER + X_WIDTH, D_MODEL),
                         (SSD_D_INNER + X_WIDTH) ** -0.5),
        'diff_w_in': nrm(ks[17], (N_DIFF_LAYERS, D_MODEL, DIFF_IN_COLS), D_MODEL ** -0.5),
        'diff_lambda': nrm(ks[18], (N_DIFF_LAYERS, 4, DIFF_HEAD_DIM), 0.1),
        'diff_subln': gain(ks[19], (N_DIFF_LAYERS, DIFF_V_DIM)),
        'diff_w_out': nrm(ks[20], (N_DIFF_LAYERS, DIFF_WIDTH + X_WIDTH, D_MODEL),
                          (DIFF_WIDTH + X_WIDTH) ** -0.5),
        'x_mem_norm': gain(ks[21], (DEPTH, D_MODEL)),
        'x_w_kv': nrm(ks[22], (DEPTH, D_MODEL, 2 * X_WIDTH), D_MODEL ** -0.5),
        'mlp_w1': nrm(ks[23], (DEPTH, D_MODEL, D_FF), D_MODEL ** -0.5),
        'mlp_w2': nrm(ks[24], (DEPTH, D_FF, D_MODEL), D_FF ** -0.5),
    }


def reference(x_prompt, x_sample, mem_prompt, mem_sample, rel_bias_table,
              norm_pre_mix, norm_post_mix, norm_pre_mlp, norm_post_mlp,
              ssd_w_in, ssd_conv_w, ssd_conv_b, ssd_dt_bias, ssd_a_log, ssd_d, ssd_norm, ssd_w_out,
              diff_w_in, diff_lambda, diff_subln, diff_w_out,
              x_mem_norm, x_w_kv, mlp_w1, mlp_w2):
    params = {
        'rel_bias_table': rel_bias_table,
        'norm_pre_mix': norm_pre_mix, 'norm_post_mix': norm_post_mix,
        'norm_pre_mlp': norm_pre_mlp, 'norm_post_mlp': norm_post_mlp,
        'ssd_w_in': ssd_w_in, 'ssd_conv_w': ssd_conv_w, 'ssd_conv_b': ssd_conv_b,
        'ssd_dt_bias': ssd_dt_bias, 'ssd_a_log': ssd_a_log, 'ssd_d': ssd_d,
        'ssd_norm': ssd_norm, 'ssd_w_out': ssd_w_out,
        'diff_w_in': diff_w_in, 'diff_lambda': diff_lambda, 'diff_subln': diff_subln,
        'diff_w_out': diff_w_out,
        'x_mem_norm': x_mem_norm, 'x_w_kv': x_w_kv,
        'mlp_w1': mlp_w1, 'mlp_w2': mlp_w2,
    }
    y_prompt = trunk(x_prompt, mem_prompt, params)
    y_sample = trunk(x_sample, mem_sample, params)
    return (y_prompt, y_sample)
```

```python
import functools
import math

import jax
import jax.numpy as jnp
from jax import lax
from jax.experimental import pallas as pl
from jax.experimental.pallas import tpu as pltpu

F32 = jnp.float32
BF16 = jnp.bfloat16

EPS = 1e-6
D_MODEL = 1024
MEM_LEN = 256

SSD_D_INNER = 2048
SSD_HEAD_DIM = 64
SSD_N_HEADS = 32
SSD_N_GROUPS = 8
SSD_HEADS_PER_GROUP = SSD_N_HEADS // SSD_N_GROUPS
SSD_D_STATE = 128
SSD_BC = SSD_N_GROUPS * SSD_D_STATE
SSD_XBC = SSD_D_INNER + 2 * SSD_BC
SSD_CONV_WIDTH = 5
SSD_CHUNK = 128
SSD_GROUP_WIDTH = SSD_D_INNER // SSD_N_GROUPS
SSD_MAIN_COLS = SSD_D_INNER + SSD_XBC + 1024
SSD_DT_COLS = 2 * SSD_N_HEADS
SSD_DT_PAD = 128

DIFF_N_HEADS = 8
DIFF_HEAD_DIM = 64
DIFF_V_DIM = 128
DIFF_QK_COLS = 1024
DIFF_WIDTH = 1024
REL_BUCKETS = 32

X_N_HEADS = 4
X_HEAD_DIM = 256
X_WIDTH = 1024
D_FF = 4096

VMEM_LIMIT_BYTES = 56 * 1024 * 1024

TOKEN_TILE = 512
ATTN_TQ = 256
ATTN_TK = 256
NEG_BIG = -1e30


def _params(n_grid_dims):
    return pltpu.CompilerParams(
        dimension_semantics=("arbitrary",) * n_grid_dims,
        vmem_limit_bytes=VMEM_LIMIT_BYTES,
    )


def _rms(x):
    return x * lax.rsqrt(jnp.mean(x * x, axis=-1, keepdims=True) + EPS)


def _resident(shape):
    nd = len(shape)
    return pl.BlockSpec(shape, lambda *_: (0,) * nd, pipeline_mode=pl.Buffered(1))


def _norm_proj_kernel(x_ref, g_ref, w_ref, *o_refs, col_chunk):
    h = (_rms(x_ref[...]) * g_ref[...]).astype(BF16)
    c0 = 0
    for o_ref in o_refs:
        n = o_ref.shape[-1]
        for j in range(0, n, col_chunk):
            cw = min(col_chunk, n - j)
            o_ref[:, j:j + cw] = jnp.dot(
                h, w_ref[:, c0 + j:c0 + j + cw], preferred_element_type=F32
            ).astype(o_ref.dtype)
        c0 += n


def _norm_proj(x2d, g, w, outs, name):
    m, d = x2d.shape
    n = w.shape[1]
    assert sum(wd for wd, _ in outs) == n and m % TOKEN_TILE == 0
    return pl.pallas_call(
        functools.partial(_norm_proj_kernel, col_chunk=1024),
        grid=(m // TOKEN_TILE,),
        in_specs=[
            pl.BlockSpec((TOKEN_TILE, d), lambda i: (i, 0)),
            _resident((1, d)),
            _resident((d, n)),
        ],
        out_specs=[pl.BlockSpec((TOKEN_TILE, wd), lambda i: (i, 0)) for wd, _ in outs],
        out_shape=[jax.ShapeDtypeStruct((m, wd), dt) for wd, dt in outs],
        compiler_params=_params(1),
        name=name,
    )(x2d, g.reshape(1, d), w)


def _dt_prep_kernel(dt_ref, bias_ref, alog_ref, o_ref, *, seq):
    n = SSD_DT_COLS
    row = lax.broadcasted_iota(jnp.int32, (SSD_CHUNK, SSD_CHUNK), 0)
    col = lax.broadcasted_iota(jnp.int32, (SSD_CHUNK, SSD_CHUNK), 1)
    prefix = (row <= col).astype(F32)
    suffix = (row >= col).astype(F32)
    is_fwd = lax.broadcasted_iota(jnp.int32, (n, SSD_CHUNK), 0) < SSD_N_HEADS
    a = -jnp.exp(alog_ref[...])
    for c in range(seq // SSD_CHUNK):
        sl = slice(c * SSD_CHUNK, (c + 1) * SSD_CHUNK)
        dt = jax.nn.softplus(dt_ref[0, :, sl] + bias_ref[...])
        da = dt * a
        cum_f = jnp.dot(da, prefix, preferred_element_type=F32, precision=lax.Precision.HIGHEST)
        cum_b = jnp.dot(da, suffix, preferred_element_type=F32, precision=lax.Precision.HIGHEST)
        o_ref[0, 0:n, sl] = jnp.where(is_fwd, cum_f, cum_b)
        o_ref[0, n:2 * n, sl] = dt


def _dt_prep(dt_t, dt_bias, a_log):
    b, n, s = dt_t.shape
    return pl.pallas_call(
        functools.partial(_dt_prep_kernel, seq=s),
        grid=(b,),
        in_specs=[
            pl.BlockSpec((1, n, s), lambda i: (i, 0, 0)),
            _resident((n, 1)),
            _resident((n, 1)),
        ],
        out_specs=pl.BlockSpec((1, 2 * n, s), lambda i: (i, 0, 0)),
        out_shape=jax.ShapeDtypeStruct((b, 2 * n, s), F32),
        compiler_params=_params(1),
        name="ssd_dt_prep",
    )(dt_t, dt_bias.reshape(n, 1), a_log.reshape(n, 1))


def _ssd_kernel(z_ref, x_ref, b_ref, c_ref, acol_ref, arow_ref,
                cwx_ref, cwb_ref, cwc_ref, cbx_ref, cbb_ref, cbc_ref, dskip_ref, ng_ref,
                o_ref, xs_scr, bs_scr, cs_scr, y_scr, *, seq):
    L = SSD_CHUNK
    R = SSD_HEADS_PER_GROUP
    GW = SSD_GROUP_WIDTH
    nc = seq // L
    halo = 16

    def conv_silu(ref, cw_ref, cb_ref, c, c0):
        main = ref[0, pl.ds(c0, L), :].astype(F32)
        p0 = pl.multiple_of(jnp.maximum(c0 - halo, 0), halo)
        n0 = pl.multiple_of(jnp.minimum(c0 + L, seq - halo), halo)
        prev = jnp.where(c > 0, ref[0, pl.ds(p0, halo), :].astype(F32), 0.0)
        nxt = jnp.where(c < nc - 1, ref[0, pl.ds(n0, halo), :].astype(F32), 0.0)
        xe = jnp.concatenate([prev, main, nxt], axis=0)
        acc = jnp.broadcast_to(cb_ref[...], main.shape)
        lo = halo - SSD_CONV_WIDTH // 2
        for w in range(SSD_CONV_WIDTH):
            acc = acc + xe[lo + w:lo + w + L] * cw_ref[w:w + 1, :]
        return acc * jax.nn.sigmoid(acc)

    def conv_body(c, carry):
        c0 = pl.multiple_of(c * L, L)
        xs_scr[pl.ds(c0, L), :] = conv_silu(x_ref, cwx_ref, cbx_ref, c, c0)
        bs_scr[pl.ds(c0, L), :] = conv_silu(b_ref, cwb_ref, cbb_ref, c, c0).astype(BF16)
        cs_scr[pl.ds(c0, L), :] = conv_silu(c_ref, cwc_ref, cbc_ref, c, c0).astype(BF16)
        return carry

    lax.fori_loop(0, nc, conv_body, 0)

    lane_head = lax.broadcasted_iota(jnp.int32, (1, GW), 1) // SSD_HEAD_DIM
    row_i = lax.broadcasted_iota(jnp.int32, (L, L), 0)
    col_i = lax.broadcasted_iota(jnp.int32, (L, L), 1)
    masks = (row_i >= col_i, row_i <= col_i)

    def expand(cols):
        out = jnp.broadcast_to(cols[:, 0:1], (L, GW))
        for r in range(1, R):
            out = jnp.where(lane_head == r, cols[:, r:r + 1], out)
        return out

    def chunk_step(c0, d, state):
        acol = acol_ref[0, 0, pl.ds(c0, L), :]
        arow = arow_ref[0, 0, :, pl.ds(c0, L)]
        xc = xs_scr[pl.ds(c0, L), :]
        bc = bs_scr[pl.ds(c0, L), :]
        cc = cs_scr[pl.ds(c0, L), :]
        g = lax.dot_general(cc, bc, (((1,), (1,)), ((), ())), preferred_element_type=F32)
        xb = xc.astype(BF16)
        y = None
        for r in range(R):
            k = d * R + r
            seg = acol[:, k:k + 1] - arow[k:k + 1, :]
            decay = jnp.exp(jnp.where(masks[d], seg, -jnp.inf))
            wd = (g * decay * arow[2 * R + k:2 * R + k + 1, :]).astype(BF16)
            xr = jnp.where(lane_head == r, xb, jnp.zeros_like(xb))
            yr = jnp.dot(wd, xr, preferred_element_type=F32)
            y = yr if y is None else y + yr
        cum_e = expand(acol[:, d * R:(d + 1) * R])
        dt_e = expand(acol[:, 2 * R + d * R:2 * R + (d + 1) * R])
        tot = cum_e[L - 1:L, :] if d == 0 else cum_e[0:1, :]
        y = y + jnp.dot(cc, state.astype(BF16), preferred_element_type=F32) * jnp.exp(cum_e)
        xsc = (xc * (jnp.exp(tot - cum_e) * dt_e)).astype(BF16)
        state = state * jnp.exp(tot) + lax.dot_general(
            bc, xsc, (((0,), (0,)), ((), ())), preferred_element_type=F32)
        return xc, y, state

    def fwd_body(c, state):
        c0 = pl.multiple_of(c * L, L)
        _, y, state = chunk_step(c0, 0, state)
        y_scr[pl.ds(c0, L), :] = y
        return state

    lax.fori_loop(0, nc, fwd_body, jnp.zeros((SSD_D_STATE, GW), F32))

    def bwd_body(i, state):
        c0 = pl.multiple_of((nc - 1 - i) * L, L)
        xc, y, state = chunk_step(c0, 1, state)
        y = y_scr[pl.ds(c0, L), :] + y + xc * dskip_ref[...]
        z = z_ref[0, pl.ds(c0, L), :].astype(F32)
        y = y * (z * jax.nn.sigmoid(z))
        o_ref[0, pl.ds(c0, L), :] = (_rms(y) * ng_ref[...]).astype(o_ref.dtype)
        return state

    lax.fori_loop(0, nc, bwd_body, jnp.zeros((SSD_D_STATE, GW), F32))


def _ssd(u_main, a_col, a_row, conv_w, conv_b, d_skip_e, norm_g):
    b, s, _ = u_main.shape
    gw, ds = SSD_GROUP_WIDTH, SSD_D_STATE
    x_blk0 = SSD_D_INNER // gw
    b_blk0 = (2 * SSD_D_INNER) // ds
    c_blk0 = (2 * SSD_D_INNER + SSD_BC) // ds
    cw_b0 = SSD_D_INNER // ds
    cw_c0 = (SSD_D_INNER + SSD_BC) // ds
    kw = SSD_CONV_WIDTH
    return pl.pallas_call(
        functools.partial(_ssd_kernel, seq=s),
        grid=(b, SSD_N_GROUPS),
        in_specs=[
            pl.BlockSpec((1, s, gw), lambda i, g: (i, 0, g)),
            pl.BlockSpec((1, s, gw), lambda i, g: (i, 0, x_blk0 + g)),
            pl.BlockSpec((1, s, ds), lambda i, g: (i, 0, b_blk0 + g)),
            pl.BlockSpec((1, s, ds), lambda i, g: (i, 0, c_blk0 + g)),
            pl.BlockSpec((1, 1, s, 16), lambda i, g: (i, g, 0, 0)),
            pl.BlockSpec((1, 1, 16, s), lambda i, g: (i, g, 0, 0)),
            pl.BlockSpec((kw, gw), lambda i, g: (0, g)),
            pl.BlockSpec((kw, ds), lambda i, g: (0, cw_b0 + g)),
            pl.BlockSpec((kw, ds), lambda i, g: (0, cw_c0 + g)),
            pl.BlockSpec((1, gw), lambda i, g: (0, g)),
            pl.BlockSpec((1, ds), lambda i, g: (0, cw_b0 + g)),
            pl.BlockSpec((1, ds), lambda i, g: (0, cw_c0 + g)),
            pl.BlockSpec((1, gw), lambda i, g: (0, g)),
            pl.BlockSpec((1, gw), lambda i, g: (0, g)),
        ],
        out_specs=pl.BlockSpec((1, s, gw), lambda i, g: (i, 0, g)),
        out_shape=jax.ShapeDtypeStruct((b, s, SSD_D_INNER), BF16),
        scratch_shapes=[
            pltpu.VMEM((s, gw), F32),
            pltpu.VMEM((s, ds), BF16),
            pltpu.VMEM((s, ds), BF16),
            pltpu.VMEM((s, gw), F32),
        ],
        compiler_params=_params(2),
        name="ssd_scan",
    )(u_main, u_main, u_main, u_main, a_col, a_row,
      conv_w, conv_w, conv_w, conv_b, conv_b, conv_b, d_skip_e, norm_g)


_BUCKET_STEPS = (12, 16, 23, 32, 46, 64, 91)


def _bias_tile_kernel(table_ref, o_ref):
    d = pl.program_id(0) - 1
    q = lax.broadcasted_iota(jnp.int32, (ATTN_TQ, ATTN_TK), 0)
    k = lax.broadcasted_iota(jnp.int32, (ATTN_TQ, ATTN_TK), 1)
    rel = k + d * ATTN_TK - q
    n = jnp.abs(rel)
    large = jnp.full_like(n, 8)
    for step in _BUCKET_STEPS:
        large = large + (n >= step).astype(jnp.int32)
    bucket = jnp.where(rel > 0, 16, 0) + jnp.where(n < 8, n, large)
    for h in range(DIFF_N_HEADS):
        acc = jnp.zeros((ATTN_TQ, ATTN_TK), F32)
        for bkt in range(REL_BUCKETS):
            acc = jnp.where(bucket == bkt, table_ref[bkt, h], acc)
        o_ref[0, h] = acc


def _bias_tiles(table):
    return pl.pallas_call(
        _bias_tile_kernel,
        grid=(3,),
        in_specs=[pl.BlockSpec(memory_space=pltpu.SMEM)],
        out_specs=pl.BlockSpec((1, DIFF_N_HEADS, ATTN_TQ, ATTN_TK), lambda i: (i, 0, 0, 0)),
        out_shape=jax.ShapeDtypeStruct((3, DIFF_N_HEADS, ATTN_TQ, ATTN_TK), F32),
        compiler_params=_params(1),
        name="rel_bias_tiles",
    )(table)


def _diff_attn_kernel(table_ref, lam_ref, q_ref, k_ref, v_ref, bias_ref, g_ref, o_ref,
                      *, seq, lam_init):
    tq, tk = ATTN_TQ, ATTN_TK
    nk = seq // tk
    h = pl.program_id(1)
    qi = pl.program_id(2)
    lane = lax.broadcasted_iota(jnp.int32, (1, 2 * DIFF_HEAD_DIM), 1)
    q = q_ref[0]
    zero = jnp.zeros_like(q)
    q1 = jnp.where(lane < DIFF_HEAD_DIM, q, zero)
    q2 = jnp.where(lane >= DIFF_HEAD_DIM, q, zero)
    scale = DIFF_HEAD_DIM ** -0.5
    nt = (((1,), (1,)), ((), ()))

    def update(s, vt, m, l, acc):
        m_new = jnp.maximum(m, jnp.max(s, axis=-1, keepdims=True))
        alpha = jnp.exp(m - m_new)
        p = jnp.exp(s - m_new)
        l = alpha * l + jnp.sum(p, axis=-1, keepdims=True)
        acc = alpha * acc + jnp.dot(p.astype(BF16), vt, preferred_element_type=F32)
        return m_new, l, acc

    def block(kb, carry, bias):
        m1, l1, a1, m2, l2, a2 = carry
        k0 = pl.multiple_of(kb * tk, tk)
        kt = k_ref[0, pl.ds(k0, tk), :]
        vt = v_ref[0, pl.ds(k0, tk), :]
        s1 = lax.dot_general(q1, kt, nt, preferred_element_type=F32) * scale + bias
        s2 = lax.dot_general(q2, kt, nt, preferred_element_type=F32) * scale + bias
        m1, l1, a1 = update(s1, vt, m1, l1, a1)
        m2, l2, a2 = update(s2, vt, m2, l2, a2)
        return m1, l1, a1, m2, l2, a2

    def init():
        col = jnp.full((tq, 1), NEG_BIG, F32)
        return (col, jnp.zeros((tq, 1), F32), jnp.zeros((tq, DIFF_V_DIM), F32),
                col, jnp.zeros((tq, 1), F32), jnp.zeros((tq, DIFF_V_DIM), F32))

    far_lo = table_ref[REL_BUCKETS // 2 - 1, h]
    far_hi = table_ref[REL_BUCKETS - 1, h]
    carry = lax.fori_loop(0, jnp.maximum(qi - 1, 0), lambda kb, c: block(kb, c, far_lo), init())
    for d in (-1, 0, 1):
        kb = qi + d
        valid = jnp.logical_and(kb >= 0, kb < nk)
        carry = lax.cond(
            valid,
            lambda c, kb=kb, d=d: block(kb, c, bias_ref[d + 1, 0]),
            lambda c: c,
            carry)
    carry = lax.fori_loop(jnp.minimum(qi + 2, nk), nk, lambda kb, c: block(kb, c, far_hi), carry)
    m1, l1, a1, m2, l2, a2 = carry

    lp = lam_ref[...]
    lam = (jnp.exp(jnp.sum(lp[0:1] * lp[1:2], axis=-1, keepdims=True))
           - jnp.exp(jnp.sum(lp[2:3] * lp[3:4], axis=-1, keepdims=True)) + lam_init)
    out = a1 * (1.0 / l1) - lam * (a2 * (1.0 / l2))
    out = _rms(out) * g_ref[...] * (1.0 - lam_init)
    o_ref[0] = out.astype(o_ref.dtype)


def _diff_attn(u, table, bias_tiles, lam_params, subln_g, lam_init):
    b, s, _ = u.shape
    hd2 = 2 * DIFF_HEAD_DIM
    k_blk0 = DIFF_QK_COLS // hd2
    v_blk0 = 2 * DIFF_QK_COLS // DIFF_V_DIM
    return pl.pallas_call(
        functools.partial(_diff_attn_kernel, seq=s, lam_init=lam_init),
        grid=(b, DIFF_N_HEADS, s // ATTN_TQ),
        in_specs=[
            pl.BlockSpec(memory_space=pltpu.SMEM),
            _resident((4, DIFF_HEAD_DIM)),
            pl.BlockSpec((1, ATTN_TQ, hd2), lambda i, h, j: (i, j, h)),
            pl.BlockSpec((1, s, hd2), lambda i, h, j: (i, 0, k_blk0 + h)),
            pl.BlockSpec((1, s, DIFF_V_DIM), lambda i, h, j: (i, 0, v_blk0 + h)),
            pl.BlockSpec((3, 1, ATTN_TQ, ATTN_TK), lambda i, h, j: (0, h, 0, 0)),
            _resident((1, DIFF_V_DIM)),
        ],
        out_specs=pl.BlockSpec((1, ATTN_TQ, DIFF_V_DIM), lambda i, h, j: (i, j, h)),
        out_shape=jax.ShapeDtypeStruct((b, s, DIFF_WIDTH), BF16),
        compiler_params=_params(3),
        name="diff_attn",
    )(table, lam_params, u, u, u, bias_tiles, subln_g.reshape(1, DIFF_V_DIM))


def _out_proj_kernel(mix_ref, q_ref, kv_ref, w_ref, x_ref, g_ref, o_ref, *, k_mix):
    q = q_ref[0]
    kv = kv_ref[0]
    nt = (((1,), (1,)), ((), ()))
    scale = X_HEAD_DIM ** -0.5
    o = jnp.dot(mix_ref[0], w_ref[0:k_mix, :], preferred_element_type=F32)
    for h in range(X_N_HEADS):
        cs = slice(h * X_HEAD_DIM, (h + 1) * X_HEAD_DIM)
        vs = slice(X_WIDTH + h * X_HEAD_DIM, X_WIDTH + (h + 1) * X_HEAD_DIM)
        logits = lax.dot_general(q[:, cs], kv[:, cs], nt, preferred_element_type=F32) * scale
        p = jnp.exp(logits - jnp.max(logits, axis=-1, keepdims=True))
        l = jnp.sum(p, axis=-1, keepdims=True)
        mem = jnp.dot(p.astype(BF16), kv[:, vs], preferred_element_type=F32) * (1.0 / l)
        o = o + jnp.dot(mem.astype(BF16), w_ref[k_mix + h * X_HEAD_DIM:k_mix + (h + 1) * X_HEAD_DIM, :],
                        preferred_element_type=F32)
    o_ref[0] = x_ref[0] + _rms(o) * g_ref[...]


def _out_proj(mix, u, q_blk, kv, w_out, x, g, name):
    b, s, k_mix = mix.shape
    d = x.shape[-1]
    return pl.pallas_call(
        functools.partial(_out_proj_kernel, k_mix=k_mix),
        grid=(b, s // TOKEN_TILE),
        in_specs=[
            pl.BlockSpec((1, TOKEN_TILE, k_mix), lambda i, j: (i, j, 0)),
            pl.BlockSpec((1, TOKEN_TILE, X_WIDTH), lambda i, j: (i, j, q_blk)),
            pl.BlockSpec((1, MEM_LEN, 2 * X_WIDTH), lambda i, j: (i, 0, 0)),
            _resident((k_mix + X_WIDTH, d)),
            pl.BlockSpec((1, TOKEN_TILE, d), lambda i, j: (i, j, 0)),
            _resident((1, d)),
        ],
        out_specs=pl.BlockSpec((1, TOKEN_TILE, d), lambda i, j: (i, j, 0)),
        out_shape=jax.ShapeDtypeStruct(x.shape, F32),
        compiler_params=_params(2),
        name=name,
    )(mix, u, kv, w_out, x, g.reshape(1, d))


def _mlp_kernel(x_ref, g1_ref, w1_ref, w2_ref, g2_ref, o_ref, *, ff_chunk):
    x = x_ref[...]
    h = (_rms(x) * g1_ref[...]).astype(BF16)
    f = None
    for c in range(0, D_FF, ff_chunk):
        a = jnp.dot(h, w1_ref[:, c:c + ff_chunk], preferred_element_type=F32)
        a = jnp.square(jnp.maximum(a, 0.0)).astype(BF16)
        part = jnp.dot(a, w2_ref[c:c + ff_chunk, :], preferred_element_type=F32)
        f = part if f is None else f + part
    o_ref[...] = x + _rms(f) * g2_ref[...]


def _mlp(x2d, g1, w1, w2, g2):
    m, d = x2d.shape
    return pl.pallas_call(
        functools.partial(_mlp_kernel, ff_chunk=1024),
        grid=(m // TOKEN_TILE,),
        in_specs=[
            pl.BlockSpec((TOKEN_TILE, d), lambda i: (i, 0)),
            _resident((1, d)),
            _resident((d, D_FF)),
            _resident((D_FF, d)),
            _resident((1, d)),
        ],
        out_specs=pl.BlockSpec((TOKEN_TILE, d), lambda i: (i, 0)),
        out_shape=jax.ShapeDtypeStruct((m, d), F32),
        compiler_params=_params(1),
        name="mlp",
    )(x2d, g1.reshape(1, d), w1, w2, g2.reshape(1, d))


def _lambda_init(layer_idx):
    return 0.8 - 0.6 * math.exp(-0.3 * layer_idx)


def _group_dt_layout(a, b, s):
    a = a.reshape(b, 2, 2, SSD_N_GROUPS, SSD_HEADS_PER_GROUP, s)
    return jnp.transpose(a, (0, 3, 1, 2, 4, 5)).reshape(b, SSD_N_GROUPS, 16, s)


def _trunk(x, mem, p):
    b, s, d = x.shape
    tokens = b * s

    def kv_proj(i):
        kv = _norm_proj(mem.reshape(b * MEM_LEN, d), p["x_mem_norm"][i], p["x_w_kv"][i],
                        [(2 * X_WIDTH, BF16)], name=f"kv_proj{i}")[0]
        return kv.reshape(b, MEM_LEN, 2 * X_WIDTH)

    def mlp(x, i):
        return _mlp(x.reshape(tokens, d), p["norm_pre_mlp"][i], p["mlp_w1"][i], p["mlp_w2"][i],
                    p["norm_post_mlp"][i]).reshape(b, s, d)

    u_main, dt_raw = _norm_proj(
        x.reshape(tokens, d), p["norm_pre_mix"][0], p["ssd_w_in"],
        [(SSD_MAIN_COLS, BF16), (SSD_DT_PAD, F32)], name="ssd_in_proj")
    u_main = u_main.reshape(b, s, SSD_MAIN_COLS)
    dt_t = jnp.transpose(dt_raw.reshape(b, s, SSD_DT_PAD)[:, :, :SSD_DT_COLS], (0, 2, 1))
    a_row = _group_dt_layout(_dt_prep(dt_t, p["ssd_dt_bias"], p["ssd_a_log"]), b, s)
    a_col = jnp.transpose(a_row, (0, 1, 3, 2))
    mix = _ssd(u_main, a_col, a_row, p["ssd_conv_w"], p["ssd_conv_b"], p["ssd_d_e"], p["ssd_norm"])
    x = _out_proj(mix, u_main, SSD_MAIN_COLS // X_WIDTH - 1, kv_proj(0), p["ssd_w_out"], x,
                  p["norm_post_mix"][0], name="ssd_out_proj")
    x = mlp(x, 0)

    u = _norm_proj(x.reshape(tokens, d), p["norm_pre_mix"][1], p["diff_w_in"],
                   [(p["diff_w_in"].shape[1], BF16)], name="diff_in_proj")[0]
    u = u.reshape(b, s, -1)
    mix = _diff_attn(u, p["rel_bias_table"], p["bias_tiles"], p["diff_lambda"], p["diff_subln"],
                     _lambda_init(1))
    x = _out_proj(mix, u, u.shape[-1] // X_WIDTH - 1, kv_proj(1), p["diff_w_out"], x,
                  p["norm_post_mix"][1], name="diff_out_proj")
    return mlp(x, 1)


def kernel(x_prompt, x_sample, mem_prompt, mem_sample, rel_bias_table, norm_pre_mix, norm_post_mix,
           norm_pre_mlp, norm_post_mlp, ssd_w_in, ssd_conv_w, ssd_conv_b, ssd_dt_bias, ssd_a_log, ssd_d,
           ssd_norm, ssd_w_out, diff_w_in, diff_lambda, diff_subln, diff_w_out, x_mem_norm, x_w_kv,
           mlp_w1, mlp_w2):
    w_in = ssd_w_in[0]
    mix_cols = SSD_D_INNER + SSD_XBC
    w_in = jnp.concatenate([
        w_in[:, :mix_cols], w_in[:, mix_cols + SSD_DT_COLS:], w_in[:, mix_cols:mix_cols + SSD_DT_COLS],
        jnp.zeros((D_MODEL, SSD_DT_PAD - SSD_DT_COLS), w_in.dtype)], axis=1)
    p = {
        "rel_bias_table": rel_bias_table,
        "bias_tiles": _bias_tiles(rel_bias_table),
        "norm_pre_mix": norm_pre_mix, "norm_post_mix": norm_post_mix,
        "norm_pre_mlp": norm_pre_mlp, "norm_post_mlp": norm_post_mlp,
        "ssd_w_in": w_in.astype(BF16),
        "ssd_conv_w": ssd_conv_w[0], "ssd_conv_b": ssd_conv_b[0].reshape(1, SSD_XBC),
        "ssd_dt_bias": ssd_dt_bias[0], "ssd_a_log": ssd_a_log[0],
        "ssd_d_e": jnp.repeat(ssd_d[0], SSD_HEAD_DIM).reshape(1, SSD_D_INNER),
        "ssd_norm": ssd_norm[0].reshape(1, SSD_D_INNER),
        "ssd_w_out": ssd_w_out[0].astype(BF16),
        "diff_w_in": diff_w_in[0].astype(BF16),
        "diff_lambda": diff_lambda[0], "diff_subln": diff_subln[0],
        "diff_w_out": diff_w_out[0].astype(BF16),
        "x_mem_norm": x_mem_norm, "x_w_kv": x_w_kv.astype(BF16),
        "mlp_w1": mlp_w1.astype(BF16), "mlp_w2": mlp_w2.astype(BF16),
    }
    return (_trunk(x_prompt, mem_prompt, p), _trunk(x_sample, mem_sample, p))
```

```python
import functools
import math

import jax
import jax.numpy as jnp
from jax import lax
from jax.experimental import pallas as pl
from jax.experimental.pallas import tpu as pltpu

F32 = jnp.float32
BF16 = jnp.bfloat16

EPS = 1e-6
D_MODEL = 1024
MEM_LEN = 256

SSD_D_INNER = 2048
SSD_HEAD_DIM = 64
SSD_N_HEADS = 32
SSD_N_GROUPS = 8
SSD_HEADS_PER_GROUP = SSD_N_HEADS // SSD_N_GROUPS
SSD_D_STATE = 128
SSD_BC = SSD_N_GROUPS * SSD_D_STATE
SSD_XBC = SSD_D_INNER + 2 * SSD_BC
SSD_CONV_WIDTH = 5
SSD_CHUNK = 128
SSD_GROUP_WIDTH = SSD_D_INNER // SSD_N_GROUPS
SSD_MAIN_COLS = SSD_D_INNER + SSD_XBC + 1024
SSD_DT_COLS = 2 * SSD_N_HEADS
SSD_DT_PAD = 128

DIFF_N_HEADS = 8
DIFF_HEAD_DIM = 64
DIFF_V_DIM = 128
DIFF_QK_COLS = 1024
DIFF_WIDTH = 1024
REL_BUCKETS = 32

X_N_HEADS = 4
X_HEAD_DIM = 256
X_WIDTH = 1024
D_FF = 4096

VMEM_LIMIT_BYTES = 56 * 1024 * 1024

TOKEN_TILE = 512
ATTN_TQ = 512
ATTN_TK = 256
ATTN_DOT_KEYS = 512
ATTN_OFF_LO = -2
ATTN_OFF_HI = ATTN_TQ // ATTN_TK + 1
ATTN_BIAS_TILES = ATTN_OFF_HI - ATTN_OFF_LO + 1
ATTN_BOUND_SLACK = 1.02
ATTN_MIN_DENOM = 2.0 ** -60
NEG_BIG = -1e30
LOG2E = math.log2(math.e)


def _params(n_grid_dims):
    return pltpu.CompilerParams(
        dimension_semantics=("arbitrary",) * n_grid_dims,
        vmem_limit_bytes=VMEM_LIMIT_BYTES,
    )


def _rms(x):
    return x * lax.rsqrt(jnp.mean(x * x, axis=-1, keepdims=True) + EPS)


def _resident(shape):
    nd = len(shape)
    return pl.BlockSpec(shape, lambda *_: (0,) * nd, pipeline_mode=pl.Buffered(1))


def _norm_proj_kernel(x_ref, g_ref, w_ref, *o_refs, col_chunk):
    h = (_rms(x_ref[...]) * g_ref[...]).astype(BF16)
    c0 = 0
    for o_ref in o_refs:
        n = o_ref.shape[-1]
        for j in range(0, n, col_chunk):
            cw = min(col_chunk, n - j)
            o_ref[:, j:j + cw] = jnp.dot(
                h, w_ref[:, c0 + j:c0 + j + cw], preferred_element_type=F32
            ).astype(o_ref.dtype)
        c0 += n


def _norm_proj(x2d, g, w, outs, name):
    m, d = x2d.shape
    n = w.shape[1]
    assert sum(wd for wd, _ in outs) == n and m % TOKEN_TILE == 0
    return pl.pallas_call(
        functools.partial(_norm_proj_kernel, col_chunk=1024),
        grid=(m // TOKEN_TILE,),
        in_specs=[
            pl.BlockSpec((TOKEN_TILE, d), lambda i: (i, 0)),
            _resident((1, d)),
            _resident((d, n)),
        ],
        out_specs=[pl.BlockSpec((TOKEN_TILE, wd), lambda i: (i, 0)) for wd, _ in outs],
        out_shape=[jax.ShapeDtypeStruct((m, wd), dt) for wd, dt in outs],
        compiler_params=_params(1),
        name=name,
    )(x2d, g.reshape(1, d), w)


def _dt_prep_kernel(dt_ref, bias_ref, alog_ref, o_ref, *, seq):
    n = SSD_DT_COLS
    row = lax.broadcasted_iota(jnp.int32, (SSD_CHUNK, SSD_CHUNK), 0)
    col = lax.broadcasted_iota(jnp.int32, (SSD_CHUNK, SSD_CHUNK), 1)
    prefix = (row <= col).astype(F32)
    suffix = (row >= col).astype(F32)
    is_fwd = lax.broadcasted_iota(jnp.int32, (n, SSD_CHUNK), 0) < SSD_N_HEADS
    a = -jnp.exp(alog_ref[...])
    for c in range(seq // SSD_CHUNK):
        sl = slice(c * SSD_CHUNK, (c + 1) * SSD_CHUNK)
        dt = jax.nn.softplus(dt_ref[0, :, sl] + bias_ref[...])
        da = dt * a
        cum_f = jnp.dot(da, prefix, preferred_element_type=F32, precision=lax.Precision.HIGHEST)
        cum_b = jnp.dot(da, suffix, preferred_element_type=F32, precision=lax.Precision.HIGHEST)
        o_ref[0, 0:n, sl] = jnp.where(is_fwd, cum_f, cum_b)
        o_ref[0, n:2 * n, sl] = dt


def _dt_prep(dt_t, dt_bias, a_log):
    b, n, s = dt_t.shape
    return pl.pallas_call(
        functools.partial(_dt_prep_kernel, seq=s),
        grid=(b,),
        in_specs=[
            pl.BlockSpec((1, n, s), lambda i: (i, 0, 0)),
            _resident((n, 1)),
            _resident((n, 1)),
        ],
        out_specs=pl.BlockSpec((1, 2 * n, s), lambda i: (i, 0, 0)),
        out_shape=jax.ShapeDtypeStruct((b, 2 * n, s), F32),
        compiler_params=_params(1),
        name="ssd_dt_prep",
    )(dt_t, dt_bias.reshape(n, 1), a_log.reshape(n, 1))


def _ssd_kernel(z_ref, x_ref, b_ref, c_ref, acol_ref, arow_ref,
                cwx_ref, cwb_ref, cwc_ref, cbx_ref, cbb_ref, cbc_ref, dskip_ref, ng_ref,
                o_ref, xs_scr, bs_scr, cs_scr, y_scr, *, seq):
    L = SSD_CHUNK
    R = SSD_HEADS_PER_GROUP
    GW = SSD_GROUP_WIDTH
    nc = seq // L
    halo = 16

    def conv_silu(ref, cw_ref, cb_ref, c, c0):
        main = ref[0, pl.ds(c0, L), :].astype(F32)
        p0 = pl.multiple_of(jnp.maximum(c0 - halo, 0), halo)
        n0 = pl.multiple_of(jnp.minimum(c0 + L, seq - halo), halo)
        prev = jnp.where(c > 0, ref[0, pl.ds(p0, halo), :].astype(F32), 0.0)
        nxt = jnp.where(c < nc - 1, ref[0, pl.ds(n0, halo), :].astype(F32), 0.0)
        xe = jnp.concatenate([prev, main, nxt], axis=0)
        acc = jnp.broadcast_to(cb_ref[...], main.shape)
        lo = halo - SSD_CONV_WIDTH // 2
        for w in range(SSD_CONV_WIDTH):
            acc = acc + xe[lo + w:lo + w + L] * cw_ref[w:w + 1, :]
        return acc * jax.nn.sigmoid(acc)

    def conv_body(c, carry):
        c0 = pl.multiple_of(c * L, L)
        xs_scr[pl.ds(c0, L), :] = conv_silu(x_ref, cwx_ref, cbx_ref, c, c0)
        bs_scr[pl.ds(c0, L), :] = conv_silu(b_ref, cwb_ref, cbb_ref, c, c0).astype(BF16)
        cs_scr[pl.ds(c0, L), :] = conv_silu(c_ref, cwc_ref, cbc_ref, c, c0).astype(BF16)
        return carry

    lax.fori_loop(0, nc, conv_body, 0)

    lane_head = lax.broadcasted_iota(jnp.int32, (1, GW), 1) // SSD_HEAD_DIM
    row_i = lax.broadcasted_iota(jnp.int32, (L, L), 0)
    col_i = lax.broadcasted_iota(jnp.int32, (L, L), 1)
    masks = (row_i >= col_i, row_i <= col_i)

    def expand(cols):
        out = jnp.broadcast_to(cols[:, 0:1], (L, GW))
        for r in range(1, R):
            out = jnp.where(lane_head == r, cols[:, r:r + 1], out)
        return out

    def chunk_step(c0, d, state):
        acol = acol_ref[0, 0, pl.ds(c0, L), :]
        arow = arow_ref[0, 0, :, pl.ds(c0, L)]
        xc = xs_scr[pl.ds(c0, L), :]
        bc = bs_scr[pl.ds(c0, L), :]
        cc = cs_scr[pl.ds(c0, L), :]
        g = lax.dot_general(cc, bc, (((1,), (1,)), ((), ())), preferred_element_type=F32)
        xb = xc.astype(BF16)
        y = None
        for r in range(R):
            k = d * R + r
            seg = acol[:, k:k + 1] - arow[k:k + 1, :]
            decay = jnp.exp(jnp.where(masks[d], seg, -jnp.inf))
            wd = (g * decay * arow[2 * R + k:2 * R + k + 1, :]).astype(BF16)
            xr = jnp.where(lane_head == r, xb, jnp.zeros_like(xb))
            yr = jnp.dot(wd, xr, preferred_element_type=F32)
            y = yr if y is None else y + yr
        cum_e = expand(acol[:, d * R:(d + 1) * R])
        dt_e = expand(acol[:, 2 * R + d * R:2 * R + (d + 1) * R])
        tot = cum_e[L - 1:L, :] if d == 0 else cum_e[0:1, :]
        y = y + jnp.dot(cc, state.astype(BF16), preferred_element_type=F32) * jnp.exp(cum_e)
        xsc = (xc * (jnp.exp(tot - cum_e) * dt_e)).astype(BF16)
        state = state * jnp.exp(tot) + lax.dot_general(
            bc, xsc, (((0,), (0,)), ((), ())), preferred_element_type=F32)
        return xc, y, state

    def fwd_body(c, state):
        c0 = pl.multiple_of(c * L, L)
        _, y, state = chunk_step(c0, 0, state)
        y_scr[pl.ds(c0, L), :] = y
        return state

    lax.fori_loop(0, nc, fwd_body, jnp.zeros((SSD_D_STATE, GW), F32))

    def bwd_body(i, state):
        c0 = pl.multiple_of((nc - 1 - i) * L, L)
        xc, y, state = chunk_step(c0, 1, state)
        y = y_scr[pl.ds(c0, L), :] + y + xc * dskip_ref[...]
        z = z_ref[0, pl.ds(c0, L), :].astype(F32)
        y = y * (z * jax.nn.sigmoid(z))
        o_ref[0, pl.ds(c0, L), :] = (_rms(y) * ng_ref[...]).astype(o_ref.dtype)
        return state

    lax.fori_loop(0, nc, bwd_body, jnp.zeros((SSD_D_STATE, GW), F32))


def _ssd(u_main, a_col, a_row, conv_w, conv_b, d_skip_e, norm_g):
    b, s, _ = u_main.shape
    gw, ds = SSD_GROUP_WIDTH, SSD_D_STATE
    x_blk0 = SSD_D_INNER // gw
    b_blk0 = (2 * SSD_D_INNER) // ds
    c_blk0 = (2 * SSD_D_INNER + SSD_BC) // ds
    cw_b0 = SSD_D_INNER // ds
    cw_c0 = (SSD_D_INNER + SSD_BC) // ds
    kw = SSD_CONV_WIDTH
    return pl.pallas_call(
        functools.partial(_ssd_kernel, seq=s),
        grid=(b, SSD_N_GROUPS),
        in_specs=[
            pl.BlockSpec((1, s, gw), lambda i, g: (i, 0, g)),
            pl.BlockSpec((1, s, gw), lambda i, g: (i, 0, x_blk0 + g)),
            pl.BlockSpec((1, s, ds), lambda i, g: (i, 0, b_blk0 + g)),
            pl.BlockSpec((1, s, ds), lambda i, g: (i, 0, c_blk0 + g)),
            pl.BlockSpec((1, 1, s, 16), lambda i, g: (i, g, 0, 0)),
            pl.BlockSpec((1, 1, 16, s), lambda i, g: (i, g, 0, 0)),
            pl.BlockSpec((kw, gw), lambda i, g: (0, g)),
            pl.BlockSpec((kw, ds), lambda i, g: (0, cw_b0 + g)),
            pl.BlockSpec((kw, ds), lambda i, g: (0, cw_c0 + g)),
            pl.BlockSpec((1, gw), lambda i, g: (0, g)),
            pl.BlockSpec((1, ds), lambda i, g: (0, cw_b0 + g)),
            pl.BlockSpec((1, ds), lambda i, g: (0, cw_c0 + g)),
            pl.BlockSpec((1, gw), lambda i, g: (0, g)),
            pl.BlockSpec((1, gw), lambda i, g: (0, g)),
        ],
        out_specs=pl.BlockSpec((1, s, gw), lambda i, g: (i, 0, g)),
        out_shape=jax.ShapeDtypeStruct((b, s, SSD_D_INNER), BF16),
        scratch_shapes=[
            pltpu.VMEM((s, gw), F32),
            pltpu.VMEM((s, ds), BF16),
            pltpu.VMEM((s, ds), BF16),
            pltpu.VMEM((s, gw), F32),
        ],
        compiler_params=_params(2),
        name="ssd_scan",
    )(u_main, u_main, u_main, u_main, a_col, a_row,
      conv_w, conv_w, conv_w, conv_b, conv_b, conv_b, d_skip_e, norm_g)


_BUCKET_STEPS = (12, 16, 23, 32, 46, 64, 91)


def _bias_tile_kernel(table_ref, o_ref):
    d = pl.program_id(0) + ATTN_OFF_LO
    k = lax.broadcasted_iota(jnp.int32, (ATTN_TK, ATTN_TQ), 0)
    q = lax.broadcasted_iota(jnp.int32, (ATTN_TK, ATTN_TQ), 1)
    rel = k + d * ATTN_TK - q
    n = jnp.abs(rel)
    large = jnp.full_like(n, 8)
    for step in _BUCKET_STEPS:
        large = large + (n >= step).astype(jnp.int32)
    bucket = jnp.where(rel > 0, 16, 0) + jnp.where(n < 8, n, large)
    for h in range(DIFF_N_HEADS):
        acc = jnp.zeros((ATTN_TK, ATTN_TQ), F32)
        for bkt in range(REL_BUCKETS):
            acc = jnp.where(bucket == bkt, table_ref[bkt, h] * LOG2E, acc)
        o_ref[0, h] = acc


def _bias_tiles(table):
    return pl.pallas_call(
        _bias_tile_kernel,
        grid=(ATTN_BIAS_TILES,),
        in_specs=[pl.BlockSpec(memory_space=pltpu.SMEM)],
        out_specs=pl.BlockSpec((1, DIFF_N_HEADS, ATTN_TK, ATTN_TQ), lambda i: (i, 0, 0, 0)),
        out_shape=jax.ShapeDtypeStruct((ATTN_BIAS_TILES, DIFF_N_HEADS, ATTN_TK, ATTN_TQ), F32),
        compiler_params=_params(1),
        name="rel_bias_tiles",
    )(table)


def _diff_attn_kernel(table_ref, lam_ref, q_ref, k_ref, v_ref, bias_ref, g_ref, o_ref, vt_scr,
                      *, seq, lam_init):
    tq, tk, dk = ATTN_TQ, ATTN_TK, ATTN_DOT_KEYS
    nq = seq // tq
    h = pl.program_id(1)
    nt = (((1,), (1,)), ((), ()))
    hd = DIFF_HEAD_DIM
    lane = lax.broadcasted_iota(jnp.int32, (1, 2 * hd), 1)
    sel_row = lax.broadcasted_iota(jnp.int32, (8, 2 * hd), 0)
    sel_lane = lax.broadcasted_iota(jnp.int32, (8, 2 * hd), 1)
    sel = jnp.where(sel_row == sel_lane // hd, 1.0, 0.0).astype(BF16)

    def half_sq_norms(x):
        xf = x.astype(F32)
        return lax.dot_general(sel, (xf * xf).astype(BF16), nt, preferred_element_type=F32)

    def prep_body(c, kmax):
        c0 = pl.multiple_of(c * 128, 128)
        vt_scr[:, pl.ds(c0, 128)] = v_ref[0, pl.ds(c0, 128), :].astype(F32).T.astype(BF16)
        return jnp.maximum(kmax, half_sq_norms(k_ref[0, pl.ds(c0, 128), :]))

    kmax = lax.fori_loop(0, seq // 128, prep_body, jnp.zeros((8, 128), F32))
    kmax = jnp.max(kmax, axis=1, keepdims=True)

    bmax = table_ref[0, h]
    for bkt in range(1, REL_BUCKETS):
        bmax = jnp.maximum(bmax, table_ref[bkt, h])
    bmax = bmax * LOG2E

    lp = lam_ref[...]
    lam = (jnp.exp(jnp.sum(lp[0:1] * lp[1:2], axis=-1, keepdims=True))
           - jnp.exp(jnp.sum(lp[2:3] * lp[3:4], axis=-1, keepdims=True)) + lam_init)

    def q_body(qi, carry):
        q0 = pl.multiple_of(qi * tq, tq)
        q = q_ref[0, pl.ds(q0, tq), :]
        zero = jnp.zeros_like(q)
        q1 = jnp.where(lane < hd, q, zero)
        q2 = jnp.where(lane >= hd, q, zero)

        def scores(j):
            kt = k_ref[0, j * dk:(j + 1) * dk, :]
            kb0 = j * (dk // tk) - (tq // tk) * qi
            bias = jnp.concatenate(
                [bias_ref[jnp.clip(kb0 + i, ATTN_OFF_LO, ATTN_OFF_HI) - ATTN_OFF_LO, 0]
                 for i in range(dk // tk)], axis=0)
            return (lax.dot_general(kt, q1, nt, preferred_element_type=F32) + bias,
                    lax.dot_general(kt, q2, nt, preferred_element_type=F32) + bias)

        def accumulate(sh1, sh2):
            l1 = jnp.zeros((1, tq), F32)
            l2 = jnp.zeros((1, tq), F32)
            a1 = jnp.zeros((DIFF_V_DIM, tq), F32)
            a2 = jnp.zeros((DIFF_V_DIM, tq), F32)
            for j in range(seq // dk):
                s1, s2 = scores(j)
                vt = vt_scr[:, j * dk:(j + 1) * dk]
                p1 = jnp.exp2(s1 - sh1)
                p2 = jnp.exp2(s2 - sh2)
                l1 = l1 + jnp.sum(p1, axis=0, keepdims=True)
                l2 = l2 + jnp.sum(p2, axis=0, keepdims=True)
                a1 = a1 + jnp.dot(vt, p1.astype(BF16), preferred_element_type=F32)
                a2 = a2 + jnp.dot(vt, p2.astype(BF16), preferred_element_type=F32)
            return l1, a1, l2, a2

        def exact_max():
            m1 = jnp.full((1, tq), NEG_BIG, F32)
            m2 = jnp.full((1, tq), NEG_BIG, F32)
            for j in range(seq // dk):
                s1, s2 = scores(j)
                m1 = jnp.maximum(m1, jnp.max(s1, axis=0, keepdims=True))
                m2 = jnp.maximum(m2, jnp.max(s2, axis=0, keepdims=True))
            return m1, m2

        bound = jnp.sqrt(half_sq_norms(q) * kmax) * ATTN_BOUND_SLACK + bmax
        res = accumulate(bound[0:1], bound[1:2])
        ok = jnp.minimum(jnp.min(res[0]), jnp.min(res[2])) > ATTN_MIN_DENOM
        l1, a1, l2, a2 = lax.cond(ok, lambda: res, lambda: accumulate(*exact_max()))

        out = a1 * (1.0 / l1) - lam * (a2 * (1.0 / l2))
        out = out * lax.rsqrt(jnp.mean(out * out, axis=0, keepdims=True) + EPS)
        out = out * (g_ref[...] * (1.0 - lam_init))
        o_ref[0, pl.ds(q0, tq), :] = out.T.astype(o_ref.dtype)
        return carry

    lax.fori_loop(0, nq, q_body, 0)


def _diff_attn(u, table, bias_tiles, lam_params, subln_g, lam_init):
    b, s, _ = u.shape
    assert s % ATTN_TQ == 0 and s % ATTN_DOT_KEYS == 0 and ATTN_DOT_KEYS % ATTN_TK == 0
    hd2 = 2 * DIFF_HEAD_DIM
    k_blk0 = DIFF_QK_COLS // hd2
    v_blk0 = 2 * DIFF_QK_COLS // DIFF_V_DIM
    return pl.pallas_call(
        functools.partial(_diff_attn_kernel, seq=s, lam_init=lam_init),
        grid=(b, DIFF_N_HEADS),
        in_specs=[
            pl.BlockSpec(memory_space=pltpu.SMEM),
            _resident((4, DIFF_HEAD_DIM)),
            pl.BlockSpec((1, s, hd2), lambda i, h: (i, 0, h)),
            pl.BlockSpec((1, s, hd2), lambda i, h: (i, 0, k_blk0 + h)),
            pl.BlockSpec((1, s, DIFF_V_DIM), lambda i, h: (i, 0, v_blk0 + h)),
            pl.BlockSpec((ATTN_BIAS_TILES, 1, ATTN_TK, ATTN_TQ), lambda i, h: (0, h, 0, 0)),
            _resident((DIFF_V_DIM, 1)),
        ],
        out_specs=pl.BlockSpec((1, s, DIFF_V_DIM), lambda i, h: (i, 0, h)),
        out_shape=jax.ShapeDtypeStruct((b, s, DIFF_WIDTH), BF16),
        scratch_shapes=[pltpu.VMEM((DIFF_V_DIM, s), BF16)],
        compiler_params=_params(2),
        name="diff_attn",
    )(table, lam_params, u, u, u, bias_tiles, subln_g.reshape(DIFF_V_DIM, 1))


def _out_proj_kernel(mix_ref, q_ref, kv_ref, w_ref, x_ref, g_ref, o_ref, *, k_mix):
    q = q_ref[0]
    kv = kv_ref[0]
    nt = (((1,), (1,)), ((), ()))
    scale = X_HEAD_DIM ** -0.5
    o = jnp.dot(mix_ref[0], w_ref[0:k_mix, :], preferred_element_type=F32)
    for h in range(X_N_HEADS):
        cs = slice(h * X_HEAD_DIM, (h + 1) * X_HEAD_DIM)
        vs = slice(X_WIDTH + h * X_HEAD_DIM, X_WIDTH + (h + 1) * X_HEAD_DIM)
        logits = lax.dot_general(q[:, cs], kv[:, cs], nt, preferred_element_type=F32) * scale
        p = jnp.exp(logits - jnp.max(logits, axis=-1, keepdims=True))
        l = jnp.sum(p, axis=-1, keepdims=True)
        mem = jnp.dot(p.astype(BF16), kv[:, vs], preferred_element_type=F32) * (1.0 / l)
        o = o + jnp.dot(mem.astype(BF16), w_ref[k_mix + h * X_HEAD_DIM:k_mix + (h + 1) * X_HEAD_DIM, :],
                        preferred_element_type=F32)
    o_ref[0] = x_ref[0] + _rms(o) * g_ref[...]


def _out_proj(mix, u, q_blk, kv, w_out, x, g, name):
    b, s, k_mix = mix.shape
    d = x.shape[-1]
    return pl.pallas_call(
        functools.partial(_out_proj_kernel, k_mix=k_mix),
        grid=(b, s // TOKEN_TILE),
        in_specs=[
            pl.BlockSpec((1, TOKEN_TILE, k_mix), lambda i, j: (i, j, 0)),
            pl.BlockSpec((1, TOKEN_TILE, X_WIDTH), lambda i, j: (i, j, q_blk)),
            pl.BlockSpec((1, MEM_LEN, 2 * X_WIDTH), lambda i, j: (i, 0, 0)),
            _resident((k_mix + X_WIDTH, d)),
            pl.BlockSpec((1, TOKEN_TILE, d), lambda i, j: (i, j, 0)),
            _resident((1, d)),
        ],
        out_specs=pl.BlockSpec((1, TOKEN_TILE, d), lambda i, j: (i, j, 0)),
        out_shape=jax.ShapeDtypeStruct(x.shape, F32),
        compiler_params=_params(2),
        name=name,
    )(mix, u, kv, w_out, x, g.reshape(1, d))


def _mlp_kernel(x_ref, g1_ref, w1_ref, w2_ref, g2_ref, o_ref, *, ff_chunk):
    x = x_ref[...]
    h = (_rms(x) * g1_ref[...]).astype(BF16)
    f = None
    for c in range(0, D_FF, ff_chunk):
        a = jnp.dot(h, w1_ref[:, c:c + ff_chunk], preferred_element_type=F32)
        a = jnp.square(jnp.maximum(a, 0.0)).astype(BF16)
        part = jnp.dot(a, w2_ref[c:c + ff_chunk, :], preferred_element_type=F32)
        f = part if f is None else f + part
    o_ref[...] = x + _rms(f) * g2_ref[...]


def _mlp(x2d, g1, w1, w2, g2):
    m, d = x2d.shape
    return pl.pallas_call(
        functools.partial(_mlp_kernel, ff_chunk=1024),
        grid=(m // TOKEN_TILE,),
        in_specs=[
            pl.BlockSpec((TOKEN_TILE, d), lambda i: (i, 0)),
            _resident((1, d)),
            _resident((d, D_FF)),
            _resident((D_FF, d)),
            _resident((1, d)),
        ],
        out_specs=pl.BlockSpec((TOKEN_TILE, d), lambda i: (i, 0)),
        out_shape=jax.ShapeDtypeStruct((m, d), F32),
        compiler_params=_params(1),
        name="mlp",
    )(x2d, g1.reshape(1, d), w1, w2, g2.reshape(1, d))


def _lambda_init(layer_idx):
    return 0.8 - 0.6 * math.exp(-0.3 * layer_idx)


def _group_dt_layout(a, b, s):
    a = a.reshape(b, 2, 2, SSD_N_GROUPS, SSD_HEADS_PER_GROUP, s)
    return jnp.transpose(a, (0, 3, 1, 2, 4, 5)).reshape(b, SSD_N_GROUPS, 16, s)


def _trunk(x, mem, p):
    b, s, d = x.shape
    tokens = b * s

    def kv_proj(i):
        kv = _norm_proj(mem.reshape(b * MEM_LEN, d), p["x_mem_norm"][i], p["x_w_kv"][i],
                        [(2 * X_WIDTH, BF16)], name=f"kv_proj{i}")[0]
        return kv.reshape(b, MEM_LEN, 2 * X_WIDTH)

    def mlp(x, i):
        return _mlp(x.reshape(tokens, d), p["norm_pre_mlp"][i], p["mlp_w1"][i], p["mlp_w2"][i],
                    p["norm_post_mlp"][i]).reshape(b, s, d)

    u_main, dt_raw = _norm_proj(
        x.reshape(tokens, d), p["norm_pre_mix"][0], p["ssd_w_in"],
        [(SSD_MAIN_COLS, BF16), (SSD_DT_PAD, F32)], name="ssd_in_proj")
    u_main = u_main.reshape(b, s, SSD_MAIN_COLS)
    dt_t = jnp.transpose(dt_raw.reshape(b, s, SSD_DT_PAD)[:, :, :SSD_DT_COLS], (0, 2, 1))
    a_row = _group_dt_layout(_dt_prep(dt_t, p["ssd_dt_bias"], p["ssd_a_log"]), b, s)
    a_col = jnp.transpose(a_row, (0, 1, 3, 2))
    mix = _ssd(u_main, a_col, a_row, p["ssd_conv_w"], p["ssd_conv_b"], p["ssd_d_e"], p["ssd_norm"])
    x = _out_proj(mix, u_main, SSD_MAIN_COLS // X_WIDTH - 1, kv_proj(0), p["ssd_w_out"], x,
                  p["norm_post_mix"][0], name="ssd_out_proj")
    x = mlp(x, 0)

    u = _norm_proj(x.reshape(tokens, d), p["norm_pre_mix"][1], p["diff_w_in"],
                   [(p["diff_w_in"].shape[1], BF16)], name="diff_in_proj")[0]
    u = u.reshape(b, s, -1)
    mix = _diff_attn(u, p["rel_bias_table"], p["bias_tiles"], p["diff_lambda"], p["diff_subln"],
                     _lambda_init(1))
    x = _out_proj(mix, u, u.shape[-1] // X_WIDTH - 1, kv_proj(1), p["diff_w_out"], x,
                  p["norm_post_mix"][1], name="diff_out_proj")
    return mlp(x, 1)


def kernel(x_prompt, x_sample, mem_prompt, mem_sample, rel_bias_table, norm_pre_mix, norm_post_mix,
           norm_pre_mlp, norm_post_mlp, ssd_w_in, ssd_conv_w, ssd_conv_b, ssd_dt_bias, ssd_a_log, ssd_d,
           ssd_norm, ssd_w_out, diff_w_in, diff_lambda, diff_subln, diff_w_out, x_mem_norm, x_w_kv,
           mlp_w1, mlp_w2):
    w_in = ssd_w_in[0]
    mix_cols = SSD_D_INNER + SSD_XBC
    w_in = jnp.concatenate([
        w_in[:, :mix_cols], w_in[:, mix_cols + SSD_DT_COLS:], w_in[:, mix_cols:mix_cols + SSD_DT_COLS],
        jnp.zeros((D_MODEL, SSD_DT_PAD - SSD_DT_COLS), w_in.dtype)], axis=1)
    p = {
        "rel_bias_table": rel_bias_table,
        "bias_tiles": _bias_tiles(rel_bias_table),
        "norm_pre_mix": norm_pre_mix, "norm_post_mix": norm_post_mix,
        "norm_pre_mlp": norm_pre_mlp, "norm_post_mlp": norm_post_mlp,
        "ssd_w_in": w_in.astype(BF16),
        "ssd_conv_w": ssd_conv_w[0], "ssd_conv_b": ssd_conv_b[0].reshape(1, SSD_XBC),
        "ssd_dt_bias": ssd_dt_bias[0], "ssd_a_log": ssd_a_log[0],
        "ssd_d_e": jnp.repeat(ssd_d[0], SSD_HEAD_DIM).reshape(1, SSD_D_INNER),
        "ssd_norm": ssd_norm[0].reshape(1, SSD_D_INNER),
        "ssd_w_out": ssd_w_out[0].astype(BF16),
        "diff_w_in": diff_w_in[0].at[:, :DIFF_QK_COLS].multiply(DIFF_HEAD_DIM ** -0.5 * LOG2E).astype(BF16),
        "diff_lambda": diff_lambda[0], "diff_subln": diff_subln[0],
        "diff_w_out": diff_w_out[0].astype(BF16),
        "x_mem_norm": x_mem_norm, "x_w_kv": x_w_kv.astype(BF16),
        "mlp_w1": mlp_w1.astype(BF16), "mlp_w2": mlp_w2.astype(BF16),
    }
    return (_trunk(x_prompt, mem_prompt, p), _trunk(x_sample, mem_sample, p))
```

```python
import functools
import math

import jax
import jax.numpy as jnp
from jax import lax
from jax.experimental import pallas as pl
from jax.experimental.pallas import tpu as pltpu

F32 = jnp.float32
BF16 = jnp.bfloat16

EPS = 1e-6
D_MODEL = 1024
MEM_LEN = 256

SSD_D_INNER = 2048
SSD_HEAD_DIM = 64
SSD_N_HEADS = 32
SSD_N_GROUPS = 8
SSD_HEADS_PER_GROUP = SSD_N_HEADS // SSD_N_GROUPS
SSD_D_STATE = 128
SSD_BC = SSD_N_GROUPS * SSD_D_STATE
SSD_XBC = SSD_D_INNER + 2 * SSD_BC
SSD_CONV_WIDTH = 5
SSD_CHUNK = 128
SSD_GROUP_WIDTH = SSD_D_INNER // SSD_N_GROUPS
SSD_MAIN_COLS = SSD_D_INNER + SSD_XBC + 1024
SSD_DT_COLS = 2 * SSD_N_HEADS
SSD_DT_PAD = 128

DIFF_N_HEADS = 8
DIFF_HEAD_DIM = 64
DIFF_V_DIM = 128
DIFF_QK_COLS = 1024
DIFF_WIDTH = 1024
REL_BUCKETS = 32

X_N_HEADS = 4
X_HEAD_DIM = 256
X_WIDTH = 1024
D_FF = 4096

VMEM_LIMIT_BYTES = 56 * 1024 * 1024

TOKEN_TILE = 512
ATTN_TQ = 512
ATTN_TK = 256
ATTN_DOT_KEYS = 512
ATTN_OFF_LO = -2
ATTN_OFF_HI = ATTN_TQ // ATTN_TK + 1
ATTN_BIAS_TILES = ATTN_OFF_HI - ATTN_OFF_LO + 1
ATTN_BOUND_SLACK = 1.02
ATTN_MIN_DENOM = 2.0 ** -60
NEG_BIG = -1e30
LOG2E = math.log2(math.e)


def _params(n_grid_dims):
    return pltpu.CompilerParams(
        dimension_semantics=("arbitrary",) * n_grid_dims,
        vmem_limit_bytes=VMEM_LIMIT_BYTES,
    )


def _rms(x):
    return x * lax.rsqrt(jnp.mean(x * x, axis=-1, keepdims=True) + EPS)


def _silu(x):
    h = 0.5 * x
    return h + h * jnp.tanh(h)


def _resident(shape):
    nd = len(shape)
    return pl.BlockSpec(shape, lambda *_: (0,) * nd, pipeline_mode=pl.Buffered(1))


def _norm_proj_kernel(x_ref, g_ref, w_ref, *o_refs, col_chunk):
    h = (_rms(x_ref[...]) * g_ref[...]).astype(BF16)
    c0 = 0
    for o_ref in o_refs:
        n = o_ref.shape[-1]
        for j in range(0, n, col_chunk):
            cw = min(col_chunk, n - j)
            o_ref[:, j:j + cw] = jnp.dot(
                h, w_ref[:, c0 + j:c0 + j + cw], preferred_element_type=F32
            ).astype(o_ref.dtype)
        c0 += n


def _norm_proj(x2d, g, w, outs, name):
    m, d = x2d.shape
    n = w.shape[1]
    assert sum(wd for wd, _ in outs) == n and m % TOKEN_TILE == 0
    return pl.pallas_call(
        functools.partial(_norm_proj_kernel, col_chunk=1024),
        grid=(m // TOKEN_TILE,),
        in_specs=[
            pl.BlockSpec((TOKEN_TILE, d), lambda i: (i, 0)),
            _resident((1, d)),
            _resident((d, n)),
        ],
        out_specs=[pl.BlockSpec((TOKEN_TILE, wd), lambda i: (i, 0)) for wd, _ in outs],
        out_shape=[jax.ShapeDtypeStruct((m, wd), dt) for wd, dt in outs],
        compiler_params=_params(1),
        name=name,
    )(x2d, g.reshape(1, d), w)


def _dt_prep_kernel(dt_ref, bias_ref, alog_ref, col_ref, row_ref, *, seq):
    n = SSD_DT_COLS
    nh = SSD_N_HEADS
    row = lax.broadcasted_iota(jnp.int32, (SSD_CHUNK, SSD_CHUNK), 0)
    col = lax.broadcasted_iota(jnp.int32, (SSD_CHUNK, SSD_CHUNK), 1)
    prefix = (row <= col).astype(F32)
    suffix = (row >= col).astype(F32)
    is_fwd = lax.broadcasted_iota(jnp.int32, (n, SSD_CHUNK), 0) < nh
    a = -jnp.exp(alog_ref[...])
    for c in range(seq // SSD_CHUNK):
        sl = slice(c * SSD_CHUNK, (c + 1) * SSD_CHUNK)
        dt = jax.nn.softplus(dt_ref[0, :, sl] + bias_ref[...])
        da = dt * a
        cum_f = jnp.dot(da, prefix, preferred_element_type=F32, precision=lax.Precision.HIGHEST)
        cum_b = jnp.dot(da, suffix, preferred_element_type=F32, precision=lax.Precision.HIGHEST)
        cum = jnp.where(is_fwd, cum_f, cum_b)
        col_ref[0, 0:n, sl] = cum
        col_ref[0, n:2 * n, sl] = dt
        row_ref[0, 0:n, sl] = cum - jnp.log(dt)
        row_ref[0, n:n + nh, sl] = jnp.log(dt[0:nh] + dt[nh:n])
        row_ref[0, n + nh:2 * n, sl] = jnp.zeros((nh, SSD_CHUNK), F32)


def _dt_prep(dt_t, dt_bias, a_log):
    b, n, s = dt_t.shape
    return pl.pallas_call(
        functools.partial(_dt_prep_kernel, seq=s),
        grid=(b,),
        in_specs=[
            pl.BlockSpec((1, n, s), lambda i: (i, 0, 0)),
            _resident((n, 1)),
            _resident((n, 1)),
        ],
        out_specs=[pl.BlockSpec((1, 2 * n, s), lambda i: (i, 0, 0))] * 2,
        out_shape=[jax.ShapeDtypeStruct((b, 2 * n, s), F32)] * 2,
        compiler_params=_params(1),
        name="ssd_dt_prep",
    )(dt_t, dt_bias.reshape(n, 1), a_log.reshape(n, 1))


def _ssd_kernel(z_ref, x_ref, b_ref, c_ref, acol_ref, arow_ref,
                cwx_ref, cwb_ref, cwc_ref, cbx_ref, cbb_ref, cbc_ref, dskip_ref, ng_ref,
                o_ref, dec_scr, cs_scr, y_scr, hf_scr, hb_scr, *, seq):
    L = SSD_CHUNK
    R = SSD_HEADS_PER_GROUP
    GW = SSD_GROUP_WIDTH
    NS = SSD_D_STATE
    KW = SSD_CONV_WIDTH
    nc = seq // L
    halo = 16
    nt = (((1,), (1,)), ((), ()))

    lane_head = lax.broadcasted_iota(jnp.int32, (1, GW), 1) // SSD_HEAD_DIM
    row_i = lax.broadcasted_iota(jnp.int32, (L, L), 0)
    col_i = lax.broadcasted_iota(jnp.int32, (L, L), 1)
    e_row = lax.broadcasted_iota(jnp.int32, (3 * 2 * R, 2 * GW), 0)
    e_col = lax.broadcasted_iota(jnp.int32, (3 * 2 * R, 2 * GW), 1)
    expand_mat = jnp.where(e_row % (2 * R) == e_col // SSD_HEAD_DIM, 1.0, 0.0).astype(BF16)

    def expand(cols):
        hi = cols.astype(BF16)
        r1 = cols - hi.astype(F32)
        mid = r1.astype(BF16)
        lo = (r1 - mid.astype(F32)).astype(BF16)
        return jnp.dot(jnp.concatenate([hi, mid, lo], axis=1), expand_mat, preferred_element_type=F32)

    E = L + 2 * halo
    s_row = lax.broadcasted_iota(jnp.int32, (L, KW * E), 0)
    s_col = lax.broadcasted_iota(jnp.int32, (L, KW * E), 1)
    shift_mat = jnp.where(s_col % E == s_row + halo + s_col // E - KW // 2, 1.0, 0.0).astype(BF16)

    def conv_stage(c):
        c0 = pl.multiple_of(c * L, L)
        p0 = pl.multiple_of(jnp.maximum(c0 - halo, 0), halo)
        n0 = pl.multiple_of(jnp.minimum(c0 + L, seq - halo), halo)

        def rows(r0, n):
            return jnp.concatenate([ref[0, pl.ds(r0, n), :] for ref in (x_ref, b_ref, c_ref)], axis=1)

        main = rows(c0, L)
        prev = rows(p0, halo)
        nxt = rows(n0, halo)
        prev = jnp.where(c > 0, prev, jnp.zeros_like(prev))
        nxt = jnp.where(c < nc - 1, nxt, jnp.zeros_like(nxt))
        xe = jnp.concatenate([prev, main, nxt], axis=0)
        cw = jnp.concatenate([cwx_ref[...], cwb_ref[...], cwc_ref[...]], axis=1).astype(BF16)
        cb = jnp.concatenate([cbx_ref[...], cbb_ref[...], cbc_ref[...]], axis=1)
        scaled = jnp.concatenate([xe * cw[w:w + 1, :] for w in range(KW)], axis=0)
        act = _silu(jnp.dot(shift_mat, scaled, preferred_element_type=F32) + cb)
        bcf = act[:, GW:GW + NS]
        return act[:, :GW], bcf.astype(BF16), act[:, GW + NS:].astype(BF16), bcf.T.astype(BF16)

    def decay_stage(c, vals):
        xc, bc, cc, bct = vals
        c0 = pl.multiple_of(c * L, L)
        g = lax.dot_general(cc, bc, nt, preferred_element_type=F32)
        acol = acol_ref[0, 0, pl.ds(c0, L), :]
        arow = arow_ref[0, 0, :, pl.ds(c0, L)]
        ws = []
        for r in range(R):
            arg = jnp.where(row_i > col_i,
                            acol[:, r:r + 1] - arow[r:r + 1, :],
                            acol[:, R + r:R + r + 1] - arow[R + r:R + r + 1, :])
            arg = jnp.where(row_i == col_i, arow[2 * R + r:2 * R + r + 1, :], arg)
            ws.append((g * jnp.exp(arg)).astype(BF16))
        return xc, cc, bct, jnp.concatenate(ws, axis=1)

    def local_stage(c, vals):
        xc, cc, bct, wcat = vals
        c0 = pl.multiple_of(c * L, L)
        cs_scr[pl.ds(c0, L), :] = cc
        acol = acol_ref[0, 0, pl.ds(c0, L), :]
        cum = acol[:, 0:2 * R]
        lane8 = lax.broadcasted_iota(jnp.int32, (1, 2 * R), 1)
        tot = jnp.where(lane8 < R, cum[L - 1:L, :], cum[0:1, :])
        ss = expand(jnp.concatenate([jnp.exp(tot - cum) * acol[:, 2 * R:4 * R],
                                     jnp.broadcast_to(jnp.exp(tot), (8, 2 * R))], axis=0))
        xb = xc.astype(BF16)
        xbd = jnp.concatenate([jnp.where(lane_head == r, xb, jnp.zeros_like(xb)) for r in range(R)], axis=0)
        y = jnp.dot(wcat, xbd, preferred_element_type=F32)
        y_scr[pl.ds(c0, L), :] = y + xc * dskip_ref[...]
        dec_scr[c] = ss[L:, :]
        xsc = (jnp.concatenate([xc, xc], axis=1) * ss[:L, :]).astype(BF16)
        s_loc = jnp.dot(bct, xsc, preferred_element_type=F32)
        hf_scr[c] = s_loc[:, :GW]
        hb_scr[c] = s_loc[:, GW:]

    def conv_pair(p):
        return conv_stage(2 * p), conv_stage(2 * p + 1)

    def decay_pair(p, vals):
        return decay_stage(2 * p, vals[0]), decay_stage(2 * p + 1, vals[1])

    def local_pair(p, vals):
        local_stage(2 * p, vals[0])
        local_stage(2 * p + 1, vals[1])

    def local_body(p, carry):
        dec_vals, conv_vals = carry
        nxt_dec = decay_pair(p + 1, conv_vals)
        nxt_conv = conv_pair(p + 2)
        local_pair(p, dec_vals)
        return nxt_dec, nxt_conv

    npair = nc // 2
    carry = (decay_pair(0, conv_pair(0)), conv_pair(1))
    dec_vals, conv_vals = lax.fori_loop(0, npair - 2, local_body, carry)
    last = decay_pair(npair - 1, conv_vals)
    local_pair(npair - 2, dec_vals)
    local_pair(npair - 1, last)

    def state_body(i, carry):
        sf, sb = carry
        cb = nc - 1 - i
        s_loc = hf_scr[i]
        hf_scr[i] = sf
        sf = sf * dec_scr[i][0:1, :GW] + s_loc
        s_loc = hb_scr[cb]
        hb_scr[cb] = sb
        sb = sb * dec_scr[cb][0:1, GW:] + s_loc
        return sf, sb

    zero = jnp.zeros((NS, GW), F32)
    lax.fori_loop(0, nc, state_body, (zero, zero))

    def out_chunk(c):
        c0 = pl.multiple_of(c * L, L)
        cc = cs_scr[pl.ds(c0, L), :]
        h = jnp.concatenate([hf_scr[c], hb_scr[c]], axis=1).astype(BF16)
        yi = jnp.dot(cc, h, preferred_element_type=F32)
        yi = yi * expand(jnp.exp(acol_ref[0, 0, pl.ds(c0, L), 0:2 * R]))
        y = y_scr[pl.ds(c0, L), :] + yi[:, :GW] + yi[:, GW:]
        y = y * _silu(z_ref[0, pl.ds(c0, L), :].astype(F32))
        o_ref[0, pl.ds(c0, L), :] = (_rms(y) * ng_ref[...]).astype(o_ref.dtype)

    def out_body(i, carry):
        out_chunk(2 * i)
        out_chunk(2 * i + 1)
        return carry

    lax.fori_loop(0, nc // 2, out_body, 0)


def _ssd(u_main, a_col, a_row, conv_w, conv_b, d_skip_e, norm_g):
    b, s, _ = u_main.shape
    gw, ds = SSD_GROUP_WIDTH, SSD_D_STATE
    nc = s // SSD_CHUNK
    assert s % (2 * SSD_CHUNK) == 0 and nc >= 4
    x_blk0 = SSD_D_INNER // gw
    b_blk0 = (2 * SSD_D_INNER) // ds
    c_blk0 = (2 * SSD_D_INNER + SSD_BC) // ds
    cw_b0 = SSD_D_INNER // ds
    cw_c0 = (SSD_D_INNER + SSD_BC) // ds
    kw = SSD_CONV_WIDTH
    return pl.pallas_call(
        functools.partial(_ssd_kernel, seq=s),
        grid=(b, SSD_N_GROUPS),
        in_specs=[
            pl.BlockSpec((1, s, gw), lambda i, g: (i, 0, g)),
            pl.BlockSpec((1, s, gw), lambda i, g: (i, 0, x_blk0 + g)),
            pl.BlockSpec((1, s, ds), lambda i, g: (i, 0, b_blk0 + g)),
            pl.BlockSpec((1, s, ds), lambda i, g: (i, 0, c_blk0 + g)),
            pl.BlockSpec((1, 1, s, 16), lambda i, g: (i, g, 0, 0)),
            pl.BlockSpec((1, 1, 16, s), lambda i, g: (i, g, 0, 0)),
            pl.BlockSpec((kw, gw), lambda i, g: (0, g)),
            pl.BlockSpec((kw, ds), lambda i, g: (0, cw_b0 + g)),
            pl.BlockSpec((kw, ds), lambda i, g: (0, cw_c0 + g)),
            pl.BlockSpec((1, gw), lambda i, g: (0, g)),
            pl.BlockSpec((1, ds), lambda i, g: (0, cw_b0 + g)),
            pl.BlockSpec((1, ds), lambda i, g: (0, cw_c0 + g)),
            pl.BlockSpec((1, gw), lambda i, g: (0, g)),
            pl.BlockSpec((1, gw), lambda i, g: (0, g)),
        ],
        out_specs=pl.BlockSpec((1, s, gw), lambda i, g: (i, 0, g)),
        out_shape=jax.ShapeDtypeStruct((b, s, SSD_D_INNER), BF16),
        scratch_shapes=[
            pltpu.VMEM((nc, 8, 2 * gw), F32),
            pltpu.VMEM((s, ds), BF16),
            pltpu.VMEM((s, gw), F32),
            pltpu.VMEM((nc, ds, gw), F32),
            pltpu.VMEM((nc, ds, gw), F32),
        ],
        compiler_params=_params(2),
        name="ssd_scan",
    )(u_main, u_main, u_main, u_main, a_col, a_row,
      conv_w, conv_w, conv_w, conv_b, conv_b, conv_b, d_skip_e, norm_g)


_BUCKET_STEPS = (12, 16, 23, 32, 46, 64, 91)


def _bias_tile_kernel(table_ref, o_ref):
    d = pl.program_id(0) + ATTN_OFF_LO
    k = lax.broadcasted_iota(jnp.int32, (ATTN_TK, ATTN_TQ), 0)
    q = lax.broadcasted_iota(jnp.int32, (ATTN_TK, ATTN_TQ), 1)
    rel = k + d * ATTN_TK - q
    n = jnp.abs(rel)
    large = jnp.full_like(n, 8)
    for step in _BUCKET_STEPS:
        large = large + (n >= step).astype(jnp.int32)
    bucket = jnp.where(rel > 0, 16, 0) + jnp.where(n < 8, n, large)
    for h in range(DIFF_N_HEADS):
        acc = jnp.zeros((ATTN_TK, ATTN_TQ), F32)
        for bkt in range(REL_BUCKETS):
            acc = jnp.where(bucket == bkt, table_ref[bkt, h] * LOG2E, acc)
        o_ref[0, h] = acc


def _bias_tiles(table):
    return pl.pallas_call(
        _bias_tile_kernel,
        grid=(ATTN_BIAS_TILES,),
        in_specs=[pl.BlockSpec(memory_space=pltpu.SMEM)],
        out_specs=pl.BlockSpec((1, DIFF_N_HEADS, ATTN_TK, ATTN_TQ), lambda i: (i, 0, 0, 0)),
        out_shape=jax.ShapeDtypeStruct((ATTN_BIAS_TILES, DIFF_N_HEADS, ATTN_TK, ATTN_TQ), F32),
        compiler_params=_params(1),
        name="rel_bias_tiles",
    )(table)


def _diff_attn_kernel(table_ref, lam_ref, q_ref, k_ref, v_ref, bias_ref, g_ref, o_ref, vt_scr,
                      *, seq, lam_init):
    tq, tk, dk = ATTN_TQ, ATTN_TK, ATTN_DOT_KEYS
    nq = seq // tq
    h = pl.program_id(1)
    nt = (((1,), (1,)), ((), ()))
    hd = DIFF_HEAD_DIM
    lane = lax.broadcasted_iota(jnp.int32, (1, 2 * hd), 1)
    sel_row = lax.broadcasted_iota(jnp.int32, (8, 2 * hd), 0)
    sel_lane = lax.broadcasted_iota(jnp.int32, (8, 2 * hd), 1)
    sel = jnp.where(sel_row == sel_lane // hd, 1.0, 0.0).astype(BF16)

    def half_sq_norms(x):
        xf = x.astype(F32)
        return lax.dot_general(sel, (xf * xf).astype(BF16), nt, preferred_element_type=F32)

    def prep_body(c, kmax):
        c0 = pl.multiple_of(c * 128, 128)
        vt_scr[:, pl.ds(c0, 128)] = v_ref[0, pl.ds(c0, 128), :].astype(F32).T.astype(BF16)
        return jnp.maximum(kmax, half_sq_norms(k_ref[0, pl.ds(c0, 128), :]))

    kmax = lax.fori_loop(0, seq // 128, prep_body, jnp.zeros((8, 128), F32))
    kmax = jnp.max(kmax, axis=1, keepdims=True)

    bmax = table_ref[0, h]
    for bkt in range(1, REL_BUCKETS):
        bmax = jnp.maximum(bmax, table_ref[bkt, h])
    bmax = bmax * LOG2E

    lp = lam_ref[...]
    lam = (jnp.exp(jnp.sum(lp[0:1] * lp[1:2], axis=-1, keepdims=True))
           - jnp.exp(jnp.sum(lp[2:3] * lp[3:4], axis=-1, keepdims=True)) + lam_init)

    def q_body(qi, carry):
        q0 = pl.multiple_of(qi * tq, tq)
        q = q_ref[0, pl.ds(q0, tq), :]
        zero = jnp.zeros_like(q)
        q1 = jnp.where(lane < hd, q, zero)
        q2 = jnp.where(lane >= hd, q, zero)

        def scores(j):
            kt = k_ref[0, j * dk:(j + 1) * dk, :]
            kb0 = j * (dk // tk) - (tq // tk) * qi
            bias = jnp.concatenate(
                [bias_ref[jnp.clip(kb0 + i, ATTN_OFF_LO, ATTN_OFF_HI) - ATTN_OFF_LO, 0]
                 for i in range(dk // tk)], axis=0)
            return (lax.dot_general(kt, q1, nt, preferred_element_type=F32) + bias,
                    lax.dot_general(kt, q2, nt, preferred_element_type=F32) + bias)

        def accumulate(sh1, sh2):
            l1 = jnp.zeros((1, tq), F32)
            l2 = jnp.zeros((1, tq), F32)
            a1 = jnp.zeros((DIFF_V_DIM, tq), F32)
            a2 = jnp.zeros((DIFF_V_DIM, tq), F32)
            for j in range(seq // dk):
                s1, s2 = scores(j)
                vt = vt_scr[:, j * dk:(j + 1) * dk]
                p1 = jnp.exp2(s1 - sh1)
                p2 = jnp.exp2(s2 - sh2)
                l1 = l1 + jnp.sum(p1, axis=0, keepdims=True)
                l2 = l2 + jnp.sum(p2, axis=0, keepdims=True)
                a1 = a1 + jnp.dot(vt, p1.astype(BF16), preferred_element_type=F32)
                a2 = a2 + jnp.dot(vt, p2.astype(BF16), preferred_element_type=F32)
            return l1, a1, l2, a2

        def exact_max():
            m1 = jnp.full((1, tq), NEG_BIG, F32)
            m2 = jnp.full((1, tq), NEG_BIG, F32)
            for j in range(seq // dk):
                s1, s2 = scores(j)
                m1 = jnp.maximum(m1, jnp.max(s1, axis=0, keepdims=True))
                m2 = jnp.maximum(m2, jnp.max(s2, axis=0, keepdims=True))
            return m1, m2

        bound = jnp.sqrt(half_sq_norms(q) * kmax) * ATTN_BOUND_SLACK + bmax
        res = accumulate(bound[0:1], bound[1:2])
        ok = jnp.minimum(jnp.min(res[0]), jnp.min(res[2])) > ATTN_MIN_DENOM
        l1, a1, l2, a2 = lax.cond(ok, lambda: res, lambda: accumulate(*exact_max()))

        out = a1 * (1.0 / l1) - lam * (a2 * (1.0 / l2))
        out = out * lax.rsqrt(jnp.mean(out * out, axis=0, keepdims=True) + EPS)
        out = out * (g_ref[...] * (1.0 - lam_init))
        o_ref[0, pl.ds(q0, tq), :] = out.T.astype(o_ref.dtype)
        return carry

    lax.fori_loop(0, nq, q_body, 0)


def _diff_attn(u, table, bias_tiles, lam_params, subln_g, lam_init):
    b, s, _ = u.shape
    assert s % ATTN_TQ == 0 and s % ATTN_DOT_KEYS == 0 and ATTN_DOT_KEYS % ATTN_TK == 0
    hd2 = 2 * DIFF_HEAD_DIM
    k_blk0 = DIFF_QK_COLS // hd2
    v_blk0 = 2 * DIFF_QK_COLS // DIFF_V_DIM
    return pl.pallas_call(
        functools.partial(_diff_attn_kernel, seq=s, lam_init=lam_init),
        grid=(b, DIFF_N_HEADS),
        in_specs=[
            pl.BlockSpec(memory_space=pltpu.SMEM),
            _resident((4, DIFF_HEAD_DIM)),
            pl.BlockSpec((1, s, hd2), lambda i, h: (i, 0, h)),
            pl.BlockSpec((1, s, hd2), lambda i, h: (i, 0, k_blk0 + h)),
            pl.BlockSpec((1, s, DIFF_V_DIM), lambda i, h: (i, 0, v_blk0 + h)),
            pl.BlockSpec((ATTN_BIAS_TILES, 1, ATTN_TK, ATTN_TQ), lambda i, h: (0, h, 0, 0)),
            _resident((DIFF_V_DIM, 1)),
        ],
        out_specs=pl.BlockSpec((1, s, DIFF_V_DIM), lambda i, h: (i, 0, h)),
        out_shape=jax.ShapeDtypeStruct((b, s, DIFF_WIDTH), BF16),
        scratch_shapes=[pltpu.VMEM((DIFF_V_DIM, s), BF16)],
        compiler_params=_params(2),
        name="diff_attn",
    )(table, lam_params, u, u, u, bias_tiles, subln_g.reshape(DIFF_V_DIM, 1))


def _out_proj_kernel(mix_ref, q_ref, kv_ref, w_ref, x_ref, g_ref, o_ref, *, k_mix):
    q = q_ref[0]
    kv = kv_ref[0]
    nt = (((1,), (1,)), ((), ()))
    scale = X_HEAD_DIM ** -0.5
    o = jnp.dot(mix_ref[0], w_ref[0:k_mix, :], preferred_element_type=F32)
    for h in range(X_N_HEADS):
        cs = slice(h * X_HEAD_DIM, (h + 1) * X_HEAD_DIM)
        vs = slice(X_WIDTH + h * X_HEAD_DIM, X_WIDTH + (h + 1) * X_HEAD_DIM)
        logits = lax.dot_general(q[:, cs], kv[:, cs], nt, preferred_element_type=F32) * scale
        p = jnp.exp(logits - jnp.max(logits, axis=-1, keepdims=True))
        l = jnp.sum(p, axis=-1, keepdims=True)
        mem = jnp.dot(p.astype(BF16), kv[:, vs], preferred_element_type=F32) * (1.0 / l)
        o = o + jnp.dot(mem.astype(BF16), w_ref[k_mix + h * X_HEAD_DIM:k_mix + (h + 1) * X_HEAD_DIM, :],
                        preferred_element_type=F32)
    o_ref[0] = x_ref[0] + _rms(o) * g_ref[...]


def _out_proj(mix, u, q_blk, kv, w_out, x, g, name):
    b, s, k_mix = mix.shape
    d = x.shape[-1]
    return pl.pallas_call(
        functools.partial(_out_proj_kernel, k_mix=k_mix),
        grid=(b, s // TOKEN_TILE),
        in_specs=[
            pl.BlockSpec((1, TOKEN_TILE, k_mix), lambda i, j: (i, j, 0)),
            pl.BlockSpec((1, TOKEN_TILE, X_WIDTH), lambda i, j: (i, j, q_blk)),
            pl.BlockSpec((1, MEM_LEN, 2 * X_WIDTH), lambda i, j: (i, 0, 0)),
            _resident((k_mix + X_WIDTH, d)),
            pl.BlockSpec((1, TOKEN_TILE, d), lambda i, j: (i, j, 0)),
            _resident((1, d)),
        ],
        out_specs=pl.BlockSpec((1, TOKEN_TILE, d), lambda i, j: (i, j, 0)),
        out_shape=jax.ShapeDtypeStruct(x.shape, F32),
        compiler_params=_params(2),
        name=name,
    )(mix, u, kv, w_out, x, g.reshape(1, d))


def _mlp_kernel(x_ref, g1_ref, w1_ref, w2_ref, g2_ref, o_ref, *, ff_chunk):
    x = x_ref[...]
    h = (_rms(x) * g1_ref[...]).astype(BF16)
    f = None
    for c in range(0, D_FF, ff_chunk):
        a = jnp.dot(h, w1_ref[:, c:c + ff_chunk], preferred_element_type=F32)
        a = jnp.square(jnp.maximum(a, 0.0)).astype(BF16)
        part = jnp.dot(a, w2_ref[c:c + ff_chunk, :], preferred_element_type=F32)
        f = part if f is None else f + part
    o_ref[...] = x + _rms(f) * g2_ref[...]


def _mlp(x2d, g1, w1, w2, g2):
    m, d = x2d.shape
    return pl.pallas_call(
        functools.partial(_mlp_kernel, ff_chunk=1024),
        grid=(m // TOKEN_TILE,),
        in_specs=[
            pl.BlockSpec((TOKEN_TILE, d), lambda i: (i, 0)),
            _resident((1, d)),
            _resident((d, D_FF)),
            _resident((D_FF, d)),
            _resident((1, d)),
        ],
        out_specs=pl.BlockSpec((TOKEN_TILE, d), lambda i: (i, 0)),
        out_shape=jax.ShapeDtypeStruct((m, d), F32),
        compiler_params=_params(1),
        name="mlp",
    )(x2d, g1.reshape(1, d), w1, w2, g2.reshape(1, d))


def _lambda_init(layer_idx):
    return 0.8 - 0.6 * math.exp(-0.3 * layer_idx)


def _group_dt_layout(a, b, s):
    a = a.reshape(b, 2, 2, SSD_N_GROUPS, SSD_HEADS_PER_GROUP, s)
    return jnp.transpose(a, (0, 3, 1, 2, 4, 5)).reshape(b, SSD_N_GROUPS, 16, s)


def _trunk(x, mem, p):
    b, s, d = x.shape
    tokens = b * s

    def kv_proj(i):
        kv = _norm_proj(mem.reshape(b * MEM_LEN, d), p["x_mem_norm"][i], p["x_w_kv"][i],
                        [(2 * X_WIDTH, BF16)], name=f"kv_proj{i}")[0]
        return kv.reshape(b, MEM_LEN, 2 * X_WIDTH)

    def mlp(x, i):
        return _mlp(x.reshape(tokens, d), p["norm_pre_mlp"][i], p["mlp_w1"][i], p["mlp_w2"][i],
                    p["norm_post_mlp"][i]).reshape(b, s, d)

    u_main, dt_raw = _norm_proj(
        x.reshape(tokens, d), p["norm_pre_mix"][0], p["ssd_w_in"],
        [(SSD_MAIN_COLS, BF16), (SSD_DT_PAD, F32)], name="ssd_in_proj")
    u_main = u_main.reshape(b, s, SSD_MAIN_COLS)
    dt_t = jnp.transpose(dt_raw.reshape(b, s, SSD_DT_PAD)[:, :, :SSD_DT_COLS], (0, 2, 1))
    col_arr, row_arr = _dt_prep(dt_t, p["ssd_dt_bias"], p["ssd_a_log"])
    a_col = jnp.transpose(_group_dt_layout(col_arr, b, s), (0, 1, 3, 2))
    a_row = _group_dt_layout(row_arr, b, s)
    mix = _ssd(u_main, a_col, a_row, p["ssd_conv_w"], p["ssd_conv_b"], p["ssd_d_e"], p["ssd_norm"])
    x = _out_proj(mix, u_main, SSD_MAIN_COLS // X_WIDTH - 1, kv_proj(0), p["ssd_w_out"], x,
                  p["norm_post_mix"][0], name="ssd_out_proj")
    x = mlp(x, 0)

    u = _norm_proj(x.reshape(tokens, d), p["norm_pre_mix"][1], p["diff_w_in"],
                   [(p["diff_w_in"].shape[1], BF16)], name="diff_in_proj")[0]
    u = u.reshape(b, s, -1)
    mix = _diff_attn(u, p["rel_bias_table"], p["bias_tiles"], p["diff_lambda"], p["diff_subln"],
                     _lambda_init(1))
    x = _out_proj(mix, u, u.shape[-1] // X_WIDTH - 1, kv_proj(1), p["diff_w_out"], x,
                  p["norm_post_mix"][1], name="diff_out_proj")
    return mlp(x, 1)


def kernel(x_prompt, x_sample, mem_prompt, mem_sample, rel_bias_table, norm_pre_mix, norm_post_mix,
           norm_pre_mlp, norm_post_mlp, ssd_w_in, ssd_conv_w, ssd_conv_b, ssd_dt_bias, ssd_a_log, ssd_d,
           ssd_norm, ssd_w_out, diff_w_in, diff_lambda, diff_subln, diff_w_out, x_mem_norm, x_w_kv,
           mlp_w1, mlp_w2):
    w_in = ssd_w_in[0]
    mix_cols = SSD_D_INNER + SSD_XBC
    w_in = jnp.concatenate([
        w_in[:, :mix_cols], w_in[:, mix_cols + SSD_DT_COLS:], w_in[:, mix_cols:mix_cols + SSD_DT_COLS],
        jnp.zeros((D_MODEL, SSD_DT_PAD - SSD_DT_COLS), w_in.dtype)], axis=1)
    p = {
        "rel_bias_table": rel_bias_table,
        "bias_tiles": _bias_tiles(rel_bias_table),
        "norm_pre_mix": norm_pre_mix, "norm_post_mix": norm_post_mix,
        "norm_pre_mlp": norm_pre_mlp, "norm_post_mlp": norm_post_mlp,
        "ssd_w_in": w_in.astype(BF16),
        "ssd_conv_w": ssd_conv_w[0], "ssd_conv_b": ssd_conv_b[0].reshape(1, SSD_XBC),
        "ssd_dt_bias": ssd_dt_bias[0], "ssd_a_log": ssd_a_log[0],
        "ssd_d_e": jnp.repeat(ssd_d[0], SSD_HEAD_DIM).reshape(1, SSD_D_INNER),
        "ssd_norm": ssd_norm[0].reshape(1, SSD_D_INNER),
        "ssd_w_out": ssd_w_out[0].astype(BF16),
        "diff_w_in": diff_w_in[0].at[:, :DIFF_QK_COLS].multiply(DIFF_HEAD_DIM ** -0.5 * LOG2E).astype(BF16),
        "diff_lambda": diff_lambda[0], "diff_subln": diff_subln[0],
        "diff_w_out": diff_w_out[0].astype(BF16),
        "x_mem_norm": x_mem_norm, "x_w_kv": x_w_kv.astype(BF16),
        "mlp_w1": mlp_w1.astype(BF16), "mlp_w2": mlp_w2.astype(BF16),
    }
    return (_trunk(x_prompt, mem_prompt, p), _trunk(x_sample, mem_sample, p))
```

```python
import functools
import math

import jax
import jax.numpy as jnp
from jax import lax
from jax.experimental import pallas as pl
from jax.experimental.pallas import tpu as pltpu

F32 = jnp.float32
BF16 = jnp.bfloat16

EPS = 1e-6
D_MODEL = 1024
MEM_LEN = 256

SSD_D_INNER = 2048
SSD_HEAD_DIM = 64
SSD_N_HEADS = 32
SSD_N_GROUPS = 8
SSD_HEADS_PER_GROUP = SSD_N_HEADS // SSD_N_GROUPS
SSD_D_STATE = 128
SSD_BC = SSD_N_GROUPS * SSD_D_STATE
SSD_XBC = SSD_D_INNER + 2 * SSD_BC
SSD_CONV_WIDTH = 5
SSD_CHUNK = 128
SSD_GROUP_WIDTH = SSD_D_INNER // SSD_N_GROUPS
SSD_MAIN_COLS = SSD_D_INNER + SSD_XBC + 1024
SSD_DT_COLS = 2 * SSD_N_HEADS
SSD_DT_PAD = 128
SSD_OUT_UNROLL = 8

DIFF_N_HEADS = 8
DIFF_HEAD_DIM = 64
DIFF_V_DIM = 128
DIFF_QK_COLS = 1024
DIFF_WIDTH = 1024
REL_BUCKETS = 32

X_N_HEADS = 4
X_HEAD_DIM = 256
X_WIDTH = 1024
D_FF = 4096

VMEM_LIMIT_BYTES = 56 * 1024 * 1024

TOKEN_TILE = 512
ATTN_TQ = 512
ATTN_TK = 256
ATTN_DOT_KEYS = 512
ATTN_OFF_LO = -2
ATTN_OFF_HI = ATTN_TQ // ATTN_TK + 1
ATTN_BIAS_TILES = ATTN_OFF_HI - ATTN_OFF_LO + 1
ATTN_BOUND_SLACK = 1.02
ATTN_MIN_DENOM = 2.0 ** -60
NEG_BIG = -1e30
LOG2E = math.log2(math.e)


def _params(n_grid_dims):
    return pltpu.CompilerParams(
        dimension_semantics=("arbitrary",) * n_grid_dims,
        vmem_limit_bytes=VMEM_LIMIT_BYTES,
    )


def _rms(x):
    return x * lax.rsqrt(jnp.mean(x * x, axis=-1, keepdims=True) + EPS)


def _silu(x):
    h = 0.5 * x
    return h + h * jnp.tanh(h)


def _resident(shape):
    nd = len(shape)
    return pl.BlockSpec(shape, lambda *_: (0,) * nd, pipeline_mode=pl.Buffered(1))


def _norm_proj_kernel(x_ref, g_ref, w_ref, *o_refs, col_chunk):
    h = (_rms(x_ref[...]) * g_ref[...]).astype(BF16)
    c0 = 0
    for o_ref in o_refs:
        n = o_ref.shape[-1]
        for j in range(0, n, col_chunk):
            cw = min(col_chunk, n - j)
            o_ref[:, j:j + cw] = jnp.dot(
                h, w_ref[:, c0 + j:c0 + j + cw], preferred_element_type=F32
            ).astype(o_ref.dtype)
        c0 += n


def _norm_proj(x2d, g, w, outs, name):
    m, d = x2d.shape
    n = w.shape[1]
    assert sum(wd for wd, _ in outs) == n and m % TOKEN_TILE == 0
    return pl.pallas_call(
        functools.partial(_norm_proj_kernel, col_chunk=1024),
        grid=(m // TOKEN_TILE,),
        in_specs=[
            pl.BlockSpec((TOKEN_TILE, d), lambda i: (i, 0)),
            _resident((1, d)),
            _resident((d, n)),
        ],
        out_specs=[pl.BlockSpec((TOKEN_TILE, wd), lambda i: (i, 0)) for wd, _ in outs],
        out_shape=[jax.ShapeDtypeStruct((m, wd), dt) for wd, dt in outs],
        compiler_params=_params(1),
        name=name,
    )(x2d, g.reshape(1, d), w)


def _dt_prep_kernel(dt_ref, bias_ref, alog_ref, col_ref, row_ref, *, seq):
    n = SSD_DT_COLS
    nh = SSD_N_HEADS
    row = lax.broadcasted_iota(jnp.int32, (SSD_CHUNK, SSD_CHUNK), 0)
    col = lax.broadcasted_iota(jnp.int32, (SSD_CHUNK, SSD_CHUNK), 1)
    prefix = (row <= col).astype(F32)
    suffix = (row >= col).astype(F32)
    is_fwd = lax.broadcasted_iota(jnp.int32, (n, SSD_CHUNK), 0) < nh
    a = -jnp.exp(alog_ref[...])
    for c in range(seq // SSD_CHUNK):
        sl = slice(c * SSD_CHUNK, (c + 1) * SSD_CHUNK)
        dt = jax.nn.softplus(dt_ref[0, :, sl] + bias_ref[...])
        da = dt * a
        cum_f = jnp.dot(da, prefix, preferred_element_type=F32, precision=lax.Precision.HIGHEST)
        cum_b = jnp.dot(da, suffix, preferred_element_type=F32, precision=lax.Precision.HIGHEST)
        cum = jnp.where(is_fwd, cum_f, cum_b)
        col_ref[0, 0:n, sl] = cum
        col_ref[0, n:2 * n, sl] = dt
        row_ref[0, 0:n, sl] = cum - jnp.log(dt)
        row_ref[0, n:n + nh, sl] = jnp.log(dt[0:nh] + dt[nh:n])
        row_ref[0, n + nh:2 * n, sl] = jnp.zeros((nh, SSD_CHUNK), F32)


def _dt_prep(dt_t, dt_bias, a_log):
    b, n, s = dt_t.shape
    return pl.pallas_call(
        functools.partial(_dt_prep_kernel, seq=s),
        grid=(b,),
        in_specs=[
            pl.BlockSpec((1, n, s), lambda i: (i, 0, 0)),
            _resident((n, 1)),
            _resident((n, 1)),
        ],
        out_specs=[pl.BlockSpec((1, 2 * n, s), lambda i: (i, 0, 0))] * 2,
        out_shape=[jax.ShapeDtypeStruct((b, 2 * n, s), F32)] * 2,
        compiler_params=_params(1),
        name="ssd_dt_prep",
    )(dt_t, dt_bias.reshape(n, 1), a_log.reshape(n, 1))


def _ssd_kernel(z_ref, x_ref, b_ref, c_ref, acol_ref, arow_ref,
                cwx_ref, cwb_ref, cwc_ref, cbx_ref, cbb_ref, cbc_ref, dskip_ref, ng_ref,
                o_ref, dec_scr, cs_scr, y_scr, hf_scr, hb_scr, *, seq):
    L = SSD_CHUNK
    R = SSD_HEADS_PER_GROUP
    GW = SSD_GROUP_WIDTH
    NS = SSD_D_STATE
    KW = SSD_CONV_WIDTH
    nc = seq // L
    halo = 16
    nt = (((1,), (1,)), ((), ()))

    lane_head = lax.broadcasted_iota(jnp.int32, (1, GW), 1) // SSD_HEAD_DIM
    row_i = lax.broadcasted_iota(jnp.int32, (L, L), 0)
    col_i = lax.broadcasted_iota(jnp.int32, (L, L), 1)
    e_row = lax.broadcasted_iota(jnp.int32, (3 * 2 * R, 2 * GW), 0)
    e_col = lax.broadcasted_iota(jnp.int32, (3 * 2 * R, 2 * GW), 1)
    expand_mat = jnp.where(e_row % (2 * R) == e_col // SSD_HEAD_DIM, 1.0, 0.0).astype(BF16)

    def expand(cols):
        hi = cols.astype(BF16)
        r1 = cols - hi.astype(F32)
        mid = r1.astype(BF16)
        lo = (r1 - mid.astype(F32)).astype(BF16)
        return jnp.dot(jnp.concatenate([hi, mid, lo], axis=1), expand_mat, preferred_element_type=F32)

    mid_tap = KW // 2
    seg_rows = [L + halo] * mid_tap + [L] + [L + halo] * mid_tap
    seg_off = [halo] * mid_tap + [0] * (mid_tap + 1)
    s_row = lax.broadcasted_iota(jnp.int32, (L, sum(seg_rows)), 0)
    s_col = lax.broadcasted_iota(jnp.int32, (L, sum(seg_rows)), 1)
    hit = jnp.zeros((L, sum(seg_rows)), jnp.bool_)
    k0 = 0
    for w in range(KW):
        hit = hit | (s_col == k0 + seg_off[w] + s_row + w - mid_tap)
        k0 += seg_rows[w]
    shift_mat = jnp.where(hit, 1.0, 0.0).astype(BF16)

    def conv_stage(c):
        c0 = pl.multiple_of(c * L, L)
        p0 = pl.multiple_of(jnp.maximum(c0 - halo, 0), halo)
        n0 = pl.multiple_of(jnp.minimum(c0 + L, seq - halo), halo)

        def rows(r0, n):
            return jnp.concatenate([ref[0, pl.ds(r0, n), :] for ref in (x_ref, b_ref, c_ref)], axis=1)

        main = rows(c0, L)
        prev = rows(p0, halo)
        nxt = rows(n0, halo)
        prev = jnp.where(c > 0, prev, jnp.zeros_like(prev))
        nxt = jnp.where(c < nc - 1, nxt, jnp.zeros_like(nxt))
        before = jnp.concatenate([prev, main], axis=0)
        after = jnp.concatenate([main, nxt], axis=0)
        cw = jnp.concatenate([cwx_ref[...], cwb_ref[...], cwc_ref[...]], axis=1).astype(BF16)
        cb = jnp.concatenate([cbx_ref[...], cbb_ref[...], cbc_ref[...]], axis=1)
        segs = [before] * mid_tap + [main] + [after] * mid_tap
        scaled = jnp.concatenate([segs[w] * cw[w:w + 1, :] for w in range(KW)], axis=0)
        act = _silu(jnp.dot(shift_mat, scaled, preferred_element_type=F32) + cb)
        bcf = act[:, GW:GW + NS]
        return act[:, :GW], bcf.astype(BF16), act[:, GW + NS:].astype(BF16), bcf.T.astype(BF16)

    def decay_stage(c, vals):
        xc, bc, cc, bct = vals
        c0 = pl.multiple_of(c * L, L)
        g = lax.dot_general(cc, bc, nt, preferred_element_type=F32)
        acol = acol_ref[0, 0, pl.ds(c0, L), :]
        arow = arow_ref[0, 0, :, pl.ds(c0, L)]
        ws = []
        for r in range(R):
            arg = jnp.where(row_i > col_i,
                            acol[:, r:r + 1] - arow[r:r + 1, :],
                            acol[:, R + r:R + r + 1] - arow[R + r:R + r + 1, :])
            arg = jnp.where(row_i == col_i, arow[2 * R + r:2 * R + r + 1, :], arg)
            ws.append((g * jnp.exp(arg)).astype(BF16))
        return xc, cc, bct, jnp.concatenate(ws, axis=1)

    def local_stage(c, vals):
        xc, cc, bct, wcat = vals
        c0 = pl.multiple_of(c * L, L)
        cs_scr[pl.ds(c0, L), :] = cc
        acol = acol_ref[0, 0, pl.ds(c0, L), :]
        cum = acol[:, 0:2 * R]
        lane8 = lax.broadcasted_iota(jnp.int32, (1, 2 * R), 1)
        tot = jnp.where(lane8 < R, cum[L - 1:L, :], cum[0:1, :])
        ss = expand(jnp.concatenate([jnp.exp(tot - cum) * acol[:, 2 * R:4 * R],
                                     jnp.broadcast_to(jnp.exp(tot), (8, 2 * R))], axis=0))
        xb = xc.astype(BF16)
        xbd = jnp.concatenate([jnp.where(lane_head == r, xb, jnp.zeros_like(xb)) for r in range(R)], axis=0)
        y = jnp.dot(wcat, xbd, preferred_element_type=F32)
        y_scr[pl.ds(c0, L), :] = y + xc * dskip_ref[...]
        dec_scr[c] = ss[L:, :]
        xsc = (jnp.concatenate([xc, xc], axis=1) * ss[:L, :]).astype(BF16)
        s_loc = jnp.dot(bct, xsc, preferred_element_type=F32)
        hf_scr[c] = s_loc[:, :GW]
        hb_scr[c] = s_loc[:, GW:]

    def conv_pair(p):
        return conv_stage(2 * p), conv_stage(2 * p + 1)

    def decay_pair(p, vals):
        return decay_stage(2 * p, vals[0]), decay_stage(2 * p + 1, vals[1])

    def local_pair(p, vals):
        local_stage(2 * p, vals[0])
        local_stage(2 * p + 1, vals[1])

    def local_body(p, carry):
        dec_vals, conv_vals = carry
        nxt_dec = decay_pair(p + 1, conv_vals)
        nxt_conv = conv_pair(p + 2)
        local_pair(p, dec_vals)
        return nxt_dec, nxt_conv

    npair = nc // 2
    carry = (decay_pair(0, conv_pair(0)), conv_pair(1))
    dec_vals, conv_vals = lax.fori_loop(0, npair - 2, local_body, carry)
    last = decay_pair(npair - 1, conv_vals)
    local_pair(npair - 2, dec_vals)
    local_pair(npair - 1, last)

    def state_body(i, carry):
        sf, sb = carry
        cb = nc - 1 - i
        s_loc = hf_scr[i]
        hf_scr[i] = sf
        sf = sf * dec_scr[i][0:1, :GW] + s_loc
        s_loc = hb_scr[cb]
        hb_scr[cb] = sb
        sb = sb * dec_scr[cb][0:1, GW:] + s_loc
        return sf, sb

    zero = jnp.zeros((NS, GW), F32)
    lax.fori_loop(0, nc, state_body, (zero, zero))

    def out_chunk(c):
        c0 = pl.multiple_of(c * L, L)
        cc = cs_scr[pl.ds(c0, L), :]
        h = jnp.concatenate([hf_scr[c], hb_scr[c]], axis=1).astype(BF16)
        yi = jnp.dot(cc, h, preferred_element_type=F32)
        yi = yi * expand(jnp.exp(acol_ref[0, 0, pl.ds(c0, L), 0:2 * R]))
        y = y_scr[pl.ds(c0, L), :] + yi[:, :GW] + yi[:, GW:]
        y = y * _silu(z_ref[0, pl.ds(c0, L), :].astype(F32))
        o_ref[0, pl.ds(c0, L), :] = (_rms(y) * ng_ref[...]).astype(o_ref.dtype)

    def out_body(i, carry):
        for j in range(SSD_OUT_UNROLL):
            out_chunk(SSD_OUT_UNROLL * i + j)
        return carry

    lax.fori_loop(0, nc // SSD_OUT_UNROLL, out_body, 0)


def _ssd(u_main, a_col, a_row, conv_w, conv_b, d_skip_e, norm_g):
    b, s, _ = u_main.shape
    gw, ds = SSD_GROUP_WIDTH, SSD_D_STATE
    nc = s // SSD_CHUNK
    assert nc % SSD_OUT_UNROLL == 0 and nc % 2 == 0 and nc >= 4
    x_blk0 = SSD_D_INNER // gw
    b_blk0 = (2 * SSD_D_INNER) // ds
    c_blk0 = (2 * SSD_D_INNER + SSD_BC) // ds
    cw_b0 = SSD_D_INNER // ds
    cw_c0 = (SSD_D_INNER + SSD_BC) // ds
    kw = SSD_CONV_WIDTH
    return pl.pallas_call(
        functools.partial(_ssd_kernel, seq=s),
        grid=(b, SSD_N_GROUPS),
        in_specs=[
            pl.BlockSpec((1, s, gw), lambda i, g: (i, 0, g)),
            pl.BlockSpec((1, s, gw), lambda i, g: (i, 0, x_blk0 + g)),
            pl.BlockSpec((1, s, ds), lambda i, g: (i, 0, b_blk0 + g)),
            pl.BlockSpec((1, s, ds), lambda i, g: (i, 0, c_blk0 + g)),
            pl.BlockSpec((1, 1, s, 16), lambda i, g: (i, g, 0, 0)),
            pl.BlockSpec((1, 1, 16, s), lambda i, g: (i, g, 0, 0)),
            pl.BlockSpec((kw, gw), lambda i, g: (0, g)),
            pl.BlockSpec((kw, ds), lambda i, g: (0, cw_b0 + g)),
            pl.BlockSpec((kw, ds), lambda i, g: (0, cw_c0 + g)),
            pl.BlockSpec((1, gw), lambda i, g: (0, g)),
            pl.BlockSpec((1, ds), lambda i, g: (0, cw_b0 + g)),
            pl.BlockSpec((1, ds), lambda i, g: (0, cw_c0 + g)),
            pl.BlockSpec((1, gw), lambda i, g: (0, g)),
            pl.BlockSpec((1, gw), lambda i, g: (0, g)),
        ],
        out_specs=pl.BlockSpec((1, s, gw), lambda i, g: (i, 0, g)),
        out_shape=jax.ShapeDtypeStruct((b, s, SSD_D_INNER), BF16),
        scratch_shapes=[
            pltpu.VMEM((nc, 8, 2 * gw), F32),
            pltpu.VMEM((s, ds), BF16),
            pltpu.VMEM((s, gw), F32),
            pltpu.VMEM((nc, ds, gw), F32),
            pltpu.VMEM((nc, ds, gw), F32),
        ],
        compiler_params=_params(2),
        name="ssd_scan",
    )(u_main, u_main, u_main, u_main, a_col, a_row,
      conv_w, conv_w, conv_w, conv_b, conv_b, conv_b, d_skip_e, norm_g)


_BUCKET_STEPS = (12, 16, 23, 32, 46, 64, 91)


def _bias_tile_kernel(table_ref, o_ref):
    d = pl.program_id(0) + ATTN_OFF_LO
    k = lax.broadcasted_iota(jnp.int32, (ATTN_TK, ATTN_TQ), 0)
    q = lax.broadcasted_iota(jnp.int32, (ATTN_TK, ATTN_TQ), 1)
    rel = k + d * ATTN_TK - q
    n = jnp.abs(rel)
    large = jnp.full_like(n, 8)
    for step in _BUCKET_STEPS:
        large = large + (n >= step).astype(jnp.int32)
    bucket = jnp.where(rel > 0, 16, 0) + jnp.where(n < 8, n, large)
    for h in range(DIFF_N_HEADS):
        acc = jnp.zeros((ATTN_TK, ATTN_TQ), F32)
        for bkt in range(REL_BUCKETS):
            acc = jnp.where(bucket == bkt, table_ref[bkt, h] * LOG2E, acc)
        o_ref[0, h] = acc


def _bias_tiles(table):
    return pl.pallas_call(
        _bias_tile_kernel,
        grid=(ATTN_BIAS_TILES,),
        in_specs=[pl.BlockSpec(memory_space=pltpu.SMEM)],
        out_specs=pl.BlockSpec((1, DIFF_N_HEADS, ATTN_TK, ATTN_TQ), lambda i: (i, 0, 0, 0)),
        out_shape=jax.ShapeDtypeStruct((ATTN_BIAS_TILES, DIFF_N_HEADS, ATTN_TK, ATTN_TQ), F32),
        compiler_params=_params(1),
        name="rel_bias_tiles",
    )(table)


def _diff_attn_kernel(table_ref, lam_ref, q_ref, k_ref, v_ref, bias_ref, g_ref, o_ref, vt_scr,
                      *, seq, lam_init):
    tq, tk, dk = ATTN_TQ, ATTN_TK, ATTN_DOT_KEYS
    nq = seq // tq
    h = pl.program_id(1)
    nt = (((1,), (1,)), ((), ()))
    hd = DIFF_HEAD_DIM
    lane = lax.broadcasted_iota(jnp.int32, (1, 2 * hd), 1)
    sel_row = lax.broadcasted_iota(jnp.int32, (8, 2 * hd), 0)
    sel_lane = lax.broadcasted_iota(jnp.int32, (8, 2 * hd), 1)
    sel = jnp.where(sel_row == sel_lane // hd, 1.0, 0.0).astype(BF16)

    def half_sq_norms(x):
        xf = x.astype(F32)
        return lax.dot_general(sel, (xf * xf).astype(BF16), nt, preferred_element_type=F32)

    def prep_body(c, kmax):
        c0 = pl.multiple_of(c * 128, 128)
        vt_scr[:, pl.ds(c0, 128)] = v_ref[0, pl.ds(c0, 128), :].astype(F32).T.astype(BF16)
        return jnp.maximum(kmax, half_sq_norms(k_ref[0, pl.ds(c0, 128), :]))

    kmax = lax.fori_loop(0, seq // 128, prep_body, jnp.zeros((8, 128), F32))
    kmax = jnp.max(kmax, axis=1, keepdims=True)

    bmax = table_ref[0, h]
    for bkt in range(1, REL_BUCKETS):
        bmax = jnp.maximum(bmax, table_ref[bkt, h])
    bmax = bmax * LOG2E
    far_lo = table_ref[REL_BUCKETS // 2 - 1, h] * LOG2E
    far_hi = table_ref[REL_BUCKETS - 1, h] * LOG2E
    n_grp = seq // dk

    lp = lam_ref[...]
    lam = (jnp.exp(jnp.sum(lp[0:1] * lp[1:2], axis=-1, keepdims=True))
           - jnp.exp(jnp.sum(lp[2:3] * lp[3:4], axis=-1, keepdims=True)) + lam_init)

    def q_body(qi, carry):
        q0 = pl.multiple_of(qi * tq, tq)
        q = q_ref[0, pl.ds(q0, tq), :]
        zero = jnp.zeros_like(q)
        q1 = jnp.where(lane < hd, q, zero)
        q2 = jnp.where(lane >= hd, q, zero)

        near = (0, 1, -1)
        order = list(near) + list(range(2, n_grp - 1))

        def group(dj):
            return lax.rem(qi + dj + n_grp, n_grp)

        def qk(j):
            kt = k_ref[0, pl.ds(pl.multiple_of(j * dk, dk), dk), :]
            return (lax.dot_general(kt, q1, nt, preferred_element_type=F32),
                    lax.dot_general(kt, q2, nt, preferred_element_type=F32))

        def bias_tiles(j):
            kb0 = j * (dk // tk) - (tq // tk) * qi
            return jnp.concatenate(
                [bias_ref[jnp.clip(kb0 + i, ATTN_OFF_LO, ATTN_OFF_HI) - ATTN_OFF_LO, 0]
                 for i in range(dk // tk)], axis=0)

        def far_bias(j):
            return jnp.where(j > qi, far_hi, far_lo)

        def accumulate(sh1, sh2):
            l1 = jnp.zeros((1, tq), F32)
            l2 = jnp.zeros((1, tq), F32)
            a1 = jnp.zeros((DIFF_V_DIM, tq), F32)
            a2 = jnp.zeros((DIFF_V_DIM, tq), F32)
            nxt = qk(group(order[0]))
            for idx, dj in enumerate(order):
                j = group(dj)
                s1, s2 = nxt
                if idx + 1 < n_grp:
                    nxt = qk(group(order[idx + 1]))
                if dj in near:
                    bias = bias_tiles(j)
                    p1 = jnp.exp2(s1 + bias - sh1)
                    p2 = jnp.exp2(s2 + bias - sh2)
                else:
                    c = far_bias(j)
                    p1 = jnp.exp2(s1 - (sh1 - c))
                    p2 = jnp.exp2(s2 - (sh2 - c))
                vt = vt_scr[:, pl.ds(pl.multiple_of(j * dk, dk), dk)]
                l1 = l1 + jnp.sum(p1, axis=0, keepdims=True)
                l2 = l2 + jnp.sum(p2, axis=0, keepdims=True)
                a1 = a1 + jnp.dot(vt, p1.astype(BF16), preferred_element_type=F32)
                a2 = a2 + jnp.dot(vt, p2.astype(BF16), preferred_element_type=F32)
            return l1, a1, l2, a2

        def exact_max():
            m1 = jnp.full((1, tq), NEG_BIG, F32)
            m2 = jnp.full((1, tq), NEG_BIG, F32)
            for dj in order:
                j = group(dj)
                s1, s2 = qk(j)
                if dj in near:
                    bias = bias_tiles(j)
                    b1 = jnp.max(s1 + bias, axis=0, keepdims=True)
                    b2 = jnp.max(s2 + bias, axis=0, keepdims=True)
                else:
                    c = far_bias(j)
                    b1 = jnp.max(s1, axis=0, keepdims=True) + c
                    b2 = jnp.max(s2, axis=0, keepdims=True) + c
                m1 = jnp.maximum(m1, b1)
                m2 = jnp.maximum(m2, b2)
            return m1, m2

        bound = jnp.sqrt(half_sq_norms(q) * kmax) * ATTN_BOUND_SLACK + bmax
        res = accumulate(bound[0:1], bound[1:2])
        ok = jnp.minimum(jnp.min(res[0]), jnp.min(res[2])) > ATTN_MIN_DENOM
        l1, a1, l2, a2 = lax.cond(ok, lambda: res, lambda: accumulate(*exact_max()))

        out = a1 * (1.0 / l1) - lam * (a2 * (1.0 / l2))
        out = out * lax.rsqrt(jnp.mean(out * out, axis=0, keepdims=True) + EPS)
        out = out * (g_ref[...] * (1.0 - lam_init))
        o_ref[0, pl.ds(q0, tq), :] = out.T.astype(o_ref.dtype)
        return carry

    lax.fori_loop(0, nq, q_body, 0)


def _diff_attn(u, table, bias_tiles, lam_params, subln_g, lam_init):
    b, s, _ = u.shape
    assert ATTN_DOT_KEYS == ATTN_TQ and ATTN_DOT_KEYS % ATTN_TK == 0
    assert s % ATTN_TQ == 0 and s // ATTN_DOT_KEYS >= 4
    hd2 = 2 * DIFF_HEAD_DIM
    k_blk0 = DIFF_QK_COLS // hd2
    v_blk0 = 2 * DIFF_QK_COLS // DIFF_V_DIM
    return pl.pallas_call(
        functools.partial(_diff_attn_kernel, seq=s, lam_init=lam_init),
        grid=(b, DIFF_N_HEADS),
        in_specs=[
            pl.BlockSpec(memory_space=pltpu.SMEM),
            _resident((4, DIFF_HEAD_DIM)),
            pl.BlockSpec((1, s, hd2), lambda i, h: (i, 0, h)),
            pl.BlockSpec((1, s, hd2), lambda i, h: (i, 0, k_blk0 + h)),
            pl.BlockSpec((1, s, DIFF_V_DIM), lambda i, h: (i, 0, v_blk0 + h)),
            pl.BlockSpec((ATTN_BIAS_TILES, 1, ATTN_TK, ATTN_TQ), lambda i, h: (0, h, 0, 0)),
            _resident((DIFF_V_DIM, 1)),
        ],
        out_specs=pl.BlockSpec((1, s, DIFF_V_DIM), lambda i, h: (i, 0, h)),
        out_shape=jax.ShapeDtypeStruct((b, s, DIFF_WIDTH), BF16),
        scratch_shapes=[pltpu.VMEM((DIFF_V_DIM, s), BF16)],
        compiler_params=_params(2),
        name="diff_attn",
    )(table, lam_params, u, u, u, bias_tiles, subln_g.reshape(DIFF_V_DIM, 1))


def _out_proj_kernel(mix_ref, q_ref, kv_ref, w_ref, x_ref, g_ref, o_ref, *, k_mix):
    q = q_ref[0]
    kv = kv_ref[0]
    nt = (((1,), (1,)), ((), ()))
    scale = X_HEAD_DIM ** -0.5
    heads = [slice(h * X_HEAD_DIM, (h + 1) * X_HEAD_DIM) for h in range(X_N_HEADS)]
    logits = [lax.dot_general(q[:, cs], kv[:, cs], nt, preferred_element_type=F32) * scale for cs in heads]
    o = jnp.dot(mix_ref[0], w_ref[0:k_mix, :], preferred_element_type=F32)
    mems = []
    for h, cs in enumerate(heads):
        vs = slice(X_WIDTH + cs.start, X_WIDTH + cs.stop)
        p = jnp.exp(logits[h] - jnp.max(logits[h], axis=-1, keepdims=True))
        l = jnp.sum(p, axis=-1, keepdims=True)
        mem = jnp.dot(p.astype(BF16), kv[:, vs], preferred_element_type=F32) * (1.0 / l)
        mems.append(mem.astype(BF16))
    o = o + jnp.dot(jnp.concatenate(mems, axis=1), w_ref[k_mix:k_mix + X_WIDTH, :],
                    preferred_element_type=F32)
    o_ref[0] = x_ref[0] + _rms(o) * g_ref[...]


def _out_proj(mix, u, q_blk, kv, w_out, x, g, name):
    b, s, k_mix = mix.shape
    d = x.shape[-1]
    return pl.pallas_call(
        functools.partial(_out_proj_kernel, k_mix=k_mix),
        grid=(b, s // TOKEN_TILE),
        in_specs=[
            pl.BlockSpec((1, TOKEN_TILE, k_mix), lambda i, j: (i, j, 0)),
            pl.BlockSpec((1, TOKEN_TILE, X_WIDTH), lambda i, j: (i, j, q_blk)),
            pl.BlockSpec((1, MEM_LEN, 2 * X_WIDTH), lambda i, j: (i, 0, 0)),
            _resident((k_mix + X_WIDTH, d)),
            pl.BlockSpec((1, TOKEN_TILE, d), lambda i, j: (i, j, 0)),
            _resident((1, d)),
        ],
        out_specs=pl.BlockSpec((1, TOKEN_TILE, d), lambda i, j: (i, j, 0)),
        out_shape=jax.ShapeDtypeStruct(x.shape, F32),
        compiler_params=_params(2),
        name=name,
    )(mix, u, kv, w_out, x, g.reshape(1, d))


def _mlp_kernel(x_ref, g1_ref, w1_ref, w2_ref, g2_ref, o_ref, *, ff_chunk):
    x = x_ref[...]
    h = (_rms(x) * g1_ref[...]).astype(BF16)
    f = None
    for c in range(0, D_FF, ff_chunk):
        a = jnp.dot(h, w1_ref[:, c:c + ff_chunk], preferred_element_type=F32)
        a = jnp.square(jnp.maximum(a, 0.0)).astype(BF16)
        part = jnp.dot(a, w2_ref[c:c + ff_chunk, :], preferred_element_type=F32)
        f = part if f is None else f + part
    o_ref[...] = x + _rms(f) * g2_ref[...]


def _mlp(x2d, g1, w1, w2, g2):
    m, d = x2d.shape
    return pl.pallas_call(
        functools.partial(_mlp_kernel, ff_chunk=1024),
        grid=(m // TOKEN_TILE,),
        in_specs=[
            pl.BlockSpec((TOKEN_TILE, d), lambda i: (i, 0)),
            _resident((1, d)),
            _resident((d, D_FF)),
            _resident((D_FF, d)),
            _resident((1, d)),
        ],
        out_specs=pl.BlockSpec((TOKEN_TILE, d), lambda i: (i, 0)),
        out_shape=jax.ShapeDtypeStruct((m, d), F32),
        compiler_params=_params(1),
        name="mlp",
    )(x2d, g1.reshape(1, d), w1, w2, g2.reshape(1, d))


def _lambda_init(layer_idx):
    return 0.8 - 0.6 * math.exp(-0.3 * layer_idx)


def _group_dt_layout(a, b, s):
    a = a.reshape(b, 2, 2, SSD_N_GROUPS, SSD_HEADS_PER_GROUP, s)
    return jnp.transpose(a, (0, 3, 1, 2, 4, 5)).reshape(b, SSD_N_GROUPS, 16, s)


def _trunk(x, mem, p):
    b, s, d = x.shape
    tokens = b * s

    def kv_proj(i):
        kv = _norm_proj(mem.reshape(b * MEM_LEN, d), p["x_mem_norm"][i], p["x_w_kv"][i],
                        [(2 * X_WIDTH, BF16)], name=f"kv_proj{i}")[0]
        return kv.reshape(b, MEM_LEN, 2 * X_WIDTH)

    def mlp(x, i):
        return _mlp(x.reshape(tokens, d), p["norm_pre_mlp"][i], p["mlp_w1"][i], p["mlp_w2"][i],
                    p["norm_post_mlp"][i]).reshape(b, s, d)

    u_main, dt_raw = _norm_proj(
        x.reshape(tokens, d), p["norm_pre_mix"][0], p["ssd_w_in"],
        [(SSD_MAIN_COLS, BF16), (SSD_DT_PAD, F32)], name="ssd_in_proj")
    u_main = u_main.reshape(b, s, SSD_MAIN_COLS)
    dt_t = jnp.transpose(dt_raw.reshape(b, s, SSD_DT_PAD)[:, :, :SSD_DT_COLS], (0, 2, 1))
    col_arr, row_arr = _dt_prep(dt_t, p["ssd_dt_bias"], p["ssd_a_log"])
    a_col = jnp.transpose(_group_dt_layout(col_arr, b, s), (0, 1, 3, 2))
    a_row = _group_dt_layout(row_arr, b, s)
    mix = _ssd(u_main, a_col, a_row, p["ssd_conv_w"], p["ssd_conv_b"], p["ssd_d_e"], p["ssd_norm"])
    x = _out_proj(mix, u_main, SSD_MAIN_COLS // X_WIDTH - 1, kv_proj(0), p["ssd_w_out"], x,
                  p["norm_post_mix"][0], name="ssd_out_proj")
    x = mlp(x, 0)

    u = _norm_proj(x.reshape(tokens, d), p["norm_pre_mix"][1], p["diff_w_in"],
                   [(p["diff_w_in"].shape[1], BF16)], name="diff_in_proj")[0]
    u = u.reshape(b, s, -1)
    mix = _diff_attn(u, p["rel_bias_table"], p["bias_tiles"], p["diff_lambda"], p["diff_subln"],
                     _lambda_init(1))
    x = _out_proj(mix, u, u.shape[-1] // X_WIDTH - 1, kv_proj(1), p["diff_w_out"], x,
                  p["norm_post_mix"][1], name="diff_out_proj")
    return mlp(x, 1)


def kernel(x_prompt, x_sample, mem_prompt, mem_sample, rel_bias_table, norm_pre_mix, norm_post_mix,
           norm_pre_mlp, norm_post_mlp, ssd_w_in, ssd_conv_w, ssd_conv_b, ssd_dt_bias, ssd_a_log, ssd_d,
           ssd_norm, ssd_w_out, diff_w_in, diff_lambda, diff_subln, diff_w_out, x_mem_norm, x_w_kv,
           mlp_w1, mlp_w2):
    w_in = ssd_w_in[0]
    mix_cols = SSD_D_INNER + SSD_XBC
    w_in = jnp.concatenate([
        w_in[:, :mix_cols], w_in[:, mix_cols + SSD_DT_COLS:], w_in[:, mix_cols:mix_cols + SSD_DT_COLS],
        jnp.zeros((D_MODEL, SSD_DT_PAD - SSD_DT_COLS), w_in.dtype)], axis=1)
    p = {
        "rel_bias_table": rel_bias_table,
        "bias_tiles": _bias_tiles(rel_bias_table),
        "norm_pre_mix": norm_pre_mix, "norm_post_mix": norm_post_mix,
        "norm_pre_mlp": norm_pre_mlp, "norm_post_mlp": norm_post_mlp,
        "ssd_w_in": w_in.astype(BF16),
        "ssd_conv_w": ssd_conv_w[0], "ssd_conv_b": ssd_conv_b[0].reshape(1, SSD_XBC),
        "ssd_dt_bias": ssd_dt_bias[0], "ssd_a_log": ssd_a_log[0],
        "ssd_d_e": jnp.repeat(ssd_d[0], SSD_HEAD_DIM).reshape(1, SSD_D_INNER),
        "ssd_norm": ssd_norm[0].reshape(1, SSD_D_INNER),
        "ssd_w_out": ssd_w_out[0].astype(BF16),
        "diff_w_in": diff_w_in[0].at[:, :DIFF_QK_COLS].multiply(DIFF_HEAD_DIM ** -0.5 * LOG2E).astype(BF16),
        "diff_lambda": diff_lambda[0], "diff_subln": diff_subln[0],
        "diff_w_out": diff_w_out[0].astype(BF16),
        "x_mem_norm": x_mem_norm, "x_w_kv": x_w_kv.astype(BF16),
        "mlp_w1": mlp_w1.astype(BF16), "mlp_w2": mlp_w2.astype(BF16),
    }
    return (_trunk(x_prompt, mem_prompt, p), _trunk(x_sample, mem_sample, p))
```

```python
import functools
import math

import jax
import jax.numpy as jnp
from jax import lax
from jax.experimental import pallas as pl
from jax.experimental.pallas import tpu as pltpu

F32 = jnp.float32
BF16 = jnp.bfloat16

EPS = 1e-6
D_MODEL = 1024
MEM_LEN = 256

SSD_D_INNER = 2048
SSD_HEAD_DIM = 64
SSD_N_HEADS = 32
SSD_N_GROUPS = 8
SSD_HEADS_PER_GROUP = SSD_N_HEADS // SSD_N_GROUPS
SSD_D_STATE = 128
SSD_BC = SSD_N_GROUPS * SSD_D_STATE
SSD_XBC = SSD_D_INNER + 2 * SSD_BC
SSD_CONV_WIDTH = 5
SSD_CHUNK = 128
SSD_GROUP_WIDTH = SSD_D_INNER // SSD_N_GROUPS
SSD_MAIN_COLS = SSD_D_INNER + SSD_XBC + 1024
SSD_DT_COLS = 2 * SSD_N_HEADS
SSD_DT_PAD = 128
SSD_OUT_UNROLL = 8

DIFF_N_HEADS = 8
DIFF_HEAD_DIM = 64
DIFF_V_DIM = 128
DIFF_QK_COLS = 1024
DIFF_WIDTH = 1024
REL_BUCKETS = 32

X_N_HEADS = 4
X_HEAD_DIM = 256
X_WIDTH = 1024
D_FF = 4096

VMEM_LIMIT_BYTES = 56 * 1024 * 1024

TOKEN_TILE = 512
ATTN_TQ = 512
ATTN_TK = 256
ATTN_DOT_KEYS = 512
ATTN_Q_UNROLL = 2
ATTN_OFF_LO = -2
ATTN_OFF_HI = ATTN_TQ // ATTN_TK + 1
ATTN_BIAS_TILES = ATTN_OFF_HI - ATTN_OFF_LO + 1
ATTN_BOUND_SLACK = 1.02
ATTN_MIN_DENOM = 2.0 ** -60
NEG_BIG = -1e30
LOG2E = math.log2(math.e)


def _params(n_grid_dims):
    return pltpu.CompilerParams(
        dimension_semantics=("arbitrary",) * n_grid_dims,
        vmem_limit_bytes=VMEM_LIMIT_BYTES,
    )


def _rms(x):
    return x * lax.rsqrt(jnp.mean(x * x, axis=-1, keepdims=True) + EPS)


def _silu(x):
    h = 0.5 * x
    return h + h * jnp.tanh(h)


def _resident(shape):
    nd = len(shape)
    return pl.BlockSpec(shape, lambda *_: (0,) * nd, pipeline_mode=pl.Buffered(1))


def _norm_proj_kernel(x_ref, g_ref, w_ref, *rest, col_chunk, has_t):
    h = (_rms(x_ref[...]) * g_ref[...]).astype(BF16)
    o_refs = rest[1:] if has_t else rest
    if has_t:
        wt_ref, ot_ref = rest[0], rest[-1]
        o_refs = o_refs[:-1]
        ot_ref[...] = lax.dot_general(wt_ref[...], h, (((1,), (1,)), ((), ())),
                                      preferred_element_type=F32).astype(ot_ref.dtype)
    c0 = 0
    for o_ref in o_refs:
        n = o_ref.shape[-1]
        for j in range(0, n, col_chunk):
            cw = min(col_chunk, n - j)
            o_ref[:, j:j + cw] = jnp.dot(
                h, w_ref[:, c0 + j:c0 + j + cw], preferred_element_type=F32
            ).astype(o_ref.dtype)
        c0 += n


def _norm_proj(x2d, g, w, outs, name, w_t=None):
    m, d = x2d.shape
    n = w.shape[1]
    assert sum(wd for wd, _ in outs) == n and m % TOKEN_TILE == 0
    in_specs = [
        pl.BlockSpec((TOKEN_TILE, d), lambda i: (i, 0)),
        _resident((1, d)),
        _resident((d, n)),
    ]
    out_specs = [pl.BlockSpec((TOKEN_TILE, wd), lambda i: (i, 0)) for wd, _ in outs]
    out_shape = [jax.ShapeDtypeStruct((m, wd), dt) for wd, dt in outs]
    args = [x2d, g.reshape(1, d), w]
    if w_t is not None:
        in_specs.append(_resident(w_t.shape))
        out_specs.append(pl.BlockSpec((w_t.shape[0], TOKEN_TILE), lambda i: (0, i)))
        out_shape.append(jax.ShapeDtypeStruct((w_t.shape[0], m), BF16))
        args.append(w_t)
    return pl.pallas_call(
        functools.partial(_norm_proj_kernel, col_chunk=1024, has_t=w_t is not None),
        grid=(m // TOKEN_TILE,),
        in_specs=in_specs,
        out_specs=out_specs,
        out_shape=out_shape,
        compiler_params=_params(1),
        name=name,
    )(*args)


def _dt_prep_kernel(dt_ref, bias_ref, alog_ref, col_ref, row_ref, *, seq):
    n = SSD_DT_COLS
    nh = SSD_N_HEADS
    row = lax.broadcasted_iota(jnp.int32, (SSD_CHUNK, SSD_CHUNK), 0)
    col = lax.broadcasted_iota(jnp.int32, (SSD_CHUNK, SSD_CHUNK), 1)
    prefix = (row <= col).astype(F32)
    suffix = (row >= col).astype(F32)
    is_fwd = lax.broadcasted_iota(jnp.int32, (n, SSD_CHUNK), 0) < nh
    a = -jnp.exp(alog_ref[...])
    for c in range(seq // SSD_CHUNK):
        sl = slice(c * SSD_CHUNK, (c + 1) * SSD_CHUNK)
        dt = jax.nn.softplus(dt_ref[0, :, sl] + bias_ref[...])
        da = dt * a
        cum_f = jnp.dot(da, prefix, preferred_element_type=F32, precision=lax.Precision.HIGHEST)
        cum_b = jnp.dot(da, suffix, preferred_element_type=F32, precision=lax.Precision.HIGHEST)
        cum = jnp.where(is_fwd, cum_f, cum_b)
        col_ref[0, 0:n, sl] = cum
        col_ref[0, n:2 * n, sl] = dt
        row_ref[0, 0:n, sl] = cum - jnp.log(dt)
        row_ref[0, n:n + nh, sl] = jnp.log(dt[0:nh] + dt[nh:n])
        row_ref[0, n + nh:2 * n, sl] = jnp.zeros((nh, SSD_CHUNK), F32)


def _dt_prep(dt_t, dt_bias, a_log):
    b, n, s = dt_t.shape
    return pl.pallas_call(
        functools.partial(_dt_prep_kernel, seq=s),
        grid=(b,),
        in_specs=[
            pl.BlockSpec((1, n, s), lambda i: (i, 0, 0)),
            _resident((n, 1)),
            _resident((n, 1)),
        ],
        out_specs=[pl.BlockSpec((1, 2 * n, s), lambda i: (i, 0, 0))] * 2,
        out_shape=[jax.ShapeDtypeStruct((b, 2 * n, s), F32)] * 2,
        compiler_params=_params(1),
        name="ssd_dt_prep",
    )(dt_t, dt_bias.reshape(n, 1), a_log.reshape(n, 1))


def _ssd_kernel(z_ref, x_ref, b_ref, c_ref, acol_ref, arow_ref,
                cwx_ref, cwb_ref, cwc_ref, cbx_ref, cbb_ref, cbc_ref, dskip_ref, ng_ref,
                o_ref, dec_scr, cs_scr, y_scr, hf_scr, hb_scr, *, seq):
    L = SSD_CHUNK
    R = SSD_HEADS_PER_GROUP
    GW = SSD_GROUP_WIDTH
    NS = SSD_D_STATE
    KW = SSD_CONV_WIDTH
    nc = seq // L
    halo = 16
    nt = (((1,), (1,)), ((), ()))

    lane_head = lax.broadcasted_iota(jnp.int32, (1, GW), 1) // SSD_HEAD_DIM
    row_i = lax.broadcasted_iota(jnp.int32, (L, L), 0)
    col_i = lax.broadcasted_iota(jnp.int32, (L, L), 1)
    e_row = lax.broadcasted_iota(jnp.int32, (3 * 2 * R, 2 * GW), 0)
    e_col = lax.broadcasted_iota(jnp.int32, (3 * 2 * R, 2 * GW), 1)
    expand_mat = jnp.where(e_row % (2 * R) == e_col // SSD_HEAD_DIM, 1.0, 0.0).astype(BF16)

    def expand(cols):
        hi = cols.astype(BF16)
        r1 = cols - hi.astype(F32)
        mid = r1.astype(BF16)
        lo = (r1 - mid.astype(F32)).astype(BF16)
        return jnp.dot(jnp.concatenate([hi, mid, lo], axis=1), expand_mat, preferred_element_type=F32)

    mid_tap = KW // 2
    seg_rows = [L + halo] * mid_tap + [L] + [L + halo] * mid_tap
    seg_off = [halo] * mid_tap + [0] * (mid_tap + 1)
    s_row = lax.broadcasted_iota(jnp.int32, (L, sum(seg_rows)), 0)
    s_col = lax.broadcasted_iota(jnp.int32, (L, sum(seg_rows)), 1)
    hit = jnp.zeros((L, sum(seg_rows)), jnp.bool_)
    k0 = 0
    for w in range(KW):
        hit = hit | (s_col == k0 + seg_off[w] + s_row + w - mid_tap)
        k0 += seg_rows[w]
    shift_mat = jnp.where(hit, 1.0, 0.0).astype(BF16)

    def conv_stage(c):
        c0 = pl.multiple_of(c * L, L)
        p0 = pl.multiple_of(jnp.maximum(c0 - halo, 0), halo)
        n0 = pl.multiple_of(jnp.minimum(c0 + L, seq - halo), halo)

        def rows(r0, n):
            return jnp.concatenate([ref[0, pl.ds(r0, n), :] for ref in (x_ref, b_ref, c_ref)], axis=1)

        main = rows(c0, L)
        prev = rows(p0, halo)
        nxt = rows(n0, halo)
        prev = jnp.where(c > 0, prev, jnp.zeros_like(prev))
        nxt = jnp.where(c < nc - 1, nxt, jnp.zeros_like(nxt))
        before = jnp.concatenate([prev, main], axis=0)
        after = jnp.concatenate([main, nxt], axis=0)
        cw = jnp.concatenate([cwx_ref[...], cwb_ref[...], cwc_ref[...]], axis=1).astype(BF16)
        cb = jnp.concatenate([cbx_ref[...], cbb_ref[...], cbc_ref[...]], axis=1)
        segs = [before] * mid_tap + [main] + [after] * mid_tap
        scaled = jnp.concatenate([segs[w] * cw[w:w + 1, :] for w in range(KW)], axis=0)
        act = _silu(jnp.dot(shift_mat, scaled, preferred_element_type=F32) + cb)
        bcf = act[:, GW:GW + NS]
        return act[:, :GW], bcf.astype(BF16), act[:, GW + NS:].astype(BF16), bcf.T.astype(BF16)

    def decay_stage(c, vals):
        xc, bc, cc, bct = vals
        c0 = pl.multiple_of(c * L, L)
        g = lax.dot_general(cc, bc, nt, preferred_element_type=F32)
        acol = acol_ref[0, 0, pl.ds(c0, L), :]
        arow = arow_ref[0, 0, :, pl.ds(c0, L)]
        ws = []
        for r in range(R):
            arg = jnp.where(row_i > col_i,
                            acol[:, r:r + 1] - arow[r:r + 1, :],
                            acol[:, R + r:R + r + 1] - arow[R + r:R + r + 1, :])
            arg = jnp.where(row_i == col_i, arow[2 * R + r:2 * R + r + 1, :], arg)
            ws.append((g * jnp.exp(arg)).astype(BF16))
        return xc, cc, bct, jnp.concatenate(ws, axis=1)

    def local_stage(c, vals):
        xc, cc, bct, wcat = vals
        c0 = pl.multiple_of(c * L, L)
        cs_scr[pl.ds(c0, L), :] = cc
        acol = acol_ref[0, 0, pl.ds(c0, L), :]
        cum = acol[:, 0:2 * R]
        lane8 = lax.broadcasted_iota(jnp.int32, (1, 2 * R), 1)
        tot = jnp.where(lane8 < R, cum[L - 1:L, :], cum[0:1, :])
        ss = expand(jnp.concatenate([jnp.exp(tot - cum) * acol[:, 2 * R:4 * R],
                                     jnp.broadcast_to(jnp.exp(tot), (8, 2 * R))], axis=0))
        xb = xc.astype(BF16)
        xbd = jnp.concatenate([jnp.where(lane_head == r, xb, jnp.zeros_like(xb)) for r in range(R)], axis=0)
        y = jnp.dot(wcat, xbd, preferred_element_type=F32)
        y_scr[pl.ds(c0, L), :] = y + xc * dskip_ref[...]
        dec_scr[c] = ss[L:, :]
        xsc = (jnp.concatenate([xc, xc], axis=1) * ss[:L, :]).astype(BF16)
        s_loc = jnp.dot(bct, xsc, preferred_element_type=F32)
        hf_scr[c] = s_loc[:, :GW]
        hb_scr[c] = s_loc[:, GW:]

    def conv_pair(p):
        return conv_stage(2 * p), conv_stage(2 * p + 1)

    def decay_pair(p, vals):
        return decay_stage(2 * p, vals[0]), decay_stage(2 * p + 1, vals[1])

    def local_pair(p, vals):
        local_stage(2 * p, vals[0])
        local_stage(2 * p + 1, vals[1])

    def local_body(p, carry):
        dec_vals, conv_vals = carry
        nxt_conv = conv_pair(p + 2)
        nxt_dec = decay_pair(p + 1, conv_vals)
        local_pair(p, dec_vals)
        return nxt_dec, nxt_conv

    npair = nc // 2
    carry = (decay_pair(0, conv_pair(0)), conv_pair(1))
    dec_vals, conv_vals = lax.fori_loop(0, npair - 2, local_body, carry)
    last = decay_pair(npair - 1, conv_vals)
    local_pair(npair - 2, dec_vals)
    local_pair(npair - 1, last)

    def state_body(i, carry):
        sf, sb = carry
        cb = nc - 1 - i
        s_loc = hf_scr[i]
        hf_scr[i] = sf
        sf = sf * dec_scr[i][0:1, :GW] + s_loc
        s_loc = hb_scr[cb]
        hb_scr[cb] = sb
        sb = sb * dec_scr[cb][0:1, GW:] + s_loc
        return sf, sb

    zero = jnp.zeros((NS, GW), F32)
    lax.fori_loop(0, nc, state_body, (zero, zero))

    def out_chunk(c):
        c0 = pl.multiple_of(c * L, L)
        cc = cs_scr[pl.ds(c0, L), :]
        h = jnp.concatenate([hf_scr[c], hb_scr[c]], axis=1).astype(BF16)
        yi = jnp.dot(cc, h, preferred_element_type=F32)
        yi = yi * expand(jnp.exp(acol_ref[0, 0, pl.ds(c0, L), 0:2 * R]))
        y = y_scr[pl.ds(c0, L), :] + yi[:, :GW] + yi[:, GW:]
        y = y * _silu(z_ref[0, pl.ds(c0, L), :].astype(F32))
        o_ref[0, pl.ds(c0, L), :] = (_rms(y) * ng_ref[...]).astype(o_ref.dtype)

    def out_body(i, carry):
        for j in range(SSD_OUT_UNROLL):
            out_chunk(SSD_OUT_UNROLL * i + j)
        return carry

    lax.fori_loop(0, nc // SSD_OUT_UNROLL, out_body, 0)


def _ssd(u_main, a_col, a_row, conv_w, conv_b, d_skip_e, norm_g):
    b, s, _ = u_main.shape
    gw, ds = SSD_GROUP_WIDTH, SSD_D_STATE
    nc = s // SSD_CHUNK
    assert nc % SSD_OUT_UNROLL == 0 and nc % 2 == 0 and nc >= 4
    x_blk0 = SSD_D_INNER // gw
    b_blk0 = (2 * SSD_D_INNER) // ds
    c_blk0 = (2 * SSD_D_INNER + SSD_BC) // ds
    cw_b0 = SSD_D_INNER // ds
    cw_c0 = (SSD_D_INNER + SSD_BC) // ds
    kw = SSD_CONV_WIDTH
    return pl.pallas_call(
        functools.partial(_ssd_kernel, seq=s),
        grid=(b, SSD_N_GROUPS),
        in_specs=[
            pl.BlockSpec((1, s, gw), lambda i, g: (i, 0, g)),
            pl.BlockSpec((1, s, gw), lambda i, g: (i, 0, x_blk0 + g)),
            pl.BlockSpec((1, s, ds), lambda i, g: (i, 0, b_blk0 + g)),
            pl.BlockSpec((1, s, ds), lambda i, g: (i, 0, c_blk0 + g)),
            pl.BlockSpec((1, 1, s, 16), lambda i, g: (i, g, 0, 0)),
            pl.BlockSpec((1, 1, 16, s), lambda i, g: (i, g, 0, 0)),
            pl.BlockSpec((kw, gw), lambda i, g: (0, g)),
            pl.BlockSpec((kw, ds), lambda i, g: (0, cw_b0 + g)),
            pl.BlockSpec((kw, ds), lambda i, g: (0, cw_c0 + g)),
            pl.BlockSpec((1, gw), lambda i, g: (0, g)),
            pl.BlockSpec((1, ds), lambda i, g: (0, cw_b0 + g)),
            pl.BlockSpec((1, ds), lambda i, g: (0, cw_c0 + g)),
            pl.BlockSpec((1, gw), lambda i, g: (0, g)),
            pl.BlockSpec((1, gw), lambda i, g: (0, g)),
        ],
        out_specs=pl.BlockSpec((1, s, gw), lambda i, g: (i, 0, g)),
        out_shape=jax.ShapeDtypeStruct((b, s, SSD_D_INNER), BF16),
        scratch_shapes=[
            pltpu.VMEM((nc, 8, 2 * gw), F32),
            pltpu.VMEM((s, ds), BF16),
            pltpu.VMEM((s, gw), F32),
            pltpu.VMEM((nc, ds, gw), F32),
            pltpu.VMEM((nc, ds, gw), F32),
        ],
        compiler_params=_params(2),
        name="ssd_scan",
    )(u_main, u_main, u_main, u_main, a_col, a_row,
      conv_w, conv_w, conv_w, conv_b, conv_b, conv_b, d_skip_e, norm_g)


_BUCKET_STEPS = (12, 16, 23, 32, 46, 64, 91)


def _bias_tile_kernel(table_ref, o_ref):
    d = pl.program_id(0) + ATTN_OFF_LO
    k = lax.broadcasted_iota(jnp.int32, (ATTN_TK, ATTN_TQ), 0)
    q = lax.broadcasted_iota(jnp.int32, (ATTN_TK, ATTN_TQ), 1)
    rel = k + d * ATTN_TK - q
    n = jnp.abs(rel)
    large = jnp.full_like(n, 8)
    for step in _BUCKET_STEPS:
        large = large + (n >= step).astype(jnp.int32)
    bucket = jnp.where(rel > 0, 16, 0) + jnp.where(n < 8, n, large)
    for h in range(DIFF_N_HEADS):
        acc = jnp.zeros((ATTN_TK, ATTN_TQ), F32)
        for bkt in range(REL_BUCKETS):
            acc = jnp.where(bucket == bkt, table_ref[bkt, h] * LOG2E, acc)
        o_ref[0, h] = acc


def _bias_tiles(table):
    return pl.pallas_call(
        _bias_tile_kernel,
        grid=(ATTN_BIAS_TILES,),
        in_specs=[pl.BlockSpec(memory_space=pltpu.SMEM)],
        out_specs=pl.BlockSpec((1, DIFF_N_HEADS, ATTN_TK, ATTN_TQ), lambda i: (i, 0, 0, 0)),
        out_shape=jax.ShapeDtypeStruct((ATTN_BIAS_TILES, DIFF_N_HEADS, ATTN_TK, ATTN_TQ), F32),
        compiler_params=_params(1),
        name="rel_bias_tiles",
    )(table)


def _diff_attn_kernel(table_ref, lam_ref, q_ref, k_ref, vt_ref, bias_ref, g_ref, o_ref,
                      *, seq, lam_init):
    tq, tk, dk = ATTN_TQ, ATTN_TK, ATTN_DOT_KEYS
    nq = seq // tq
    h = pl.program_id(1)
    nt = (((1,), (1,)), ((), ()))
    hd = DIFF_HEAD_DIM
    lane = lax.broadcasted_iota(jnp.int32, (1, 2 * hd), 1)
    sel_row = lax.broadcasted_iota(jnp.int32, (8, 2 * hd), 0)
    sel_lane = lax.broadcasted_iota(jnp.int32, (8, 2 * hd), 1)
    sel = jnp.where(sel_row == sel_lane // hd, 1.0, 0.0).astype(BF16)

    def half_sq_norms(x):
        xf = x.astype(F32)
        return lax.dot_general(sel, (xf * xf).astype(BF16), nt, preferred_element_type=F32)

    kmax = half_sq_norms(k_ref[0, 0:dk, :])
    for c in range(1, seq // dk):
        kmax = jnp.maximum(kmax, half_sq_norms(k_ref[0, c * dk:(c + 1) * dk, :]))
    kmax = jnp.max(kmax, axis=1, keepdims=True)

    bmax = table_ref[0, h]
    for bkt in range(1, REL_BUCKETS):
        bmax = jnp.maximum(bmax, table_ref[bkt, h])
    bmax = bmax * LOG2E
    far_lo = table_ref[REL_BUCKETS // 2 - 1, h] * LOG2E
    far_hi = table_ref[REL_BUCKETS - 1, h] * LOG2E
    n_grp = seq // dk

    lp = lam_ref[...]
    lam = (jnp.exp(jnp.sum(lp[0:1] * lp[1:2], axis=-1, keepdims=True))
           - jnp.exp(jnp.sum(lp[2:3] * lp[3:4], axis=-1, keepdims=True)) + lam_init)

    def q_block(qi):
        q0 = pl.multiple_of(qi * tq, tq)
        q = q_ref[0, pl.ds(q0, tq), :]
        zero = jnp.zeros_like(q)
        q1 = jnp.where(lane < hd, q, zero)
        q2 = jnp.where(lane >= hd, q, zero)

        near = (0, 1, -1)
        order = list(near) + list(range(2, n_grp - 1))

        def group(dj):
            return lax.rem(qi + dj + n_grp, n_grp)

        def qk(j):
            kt = k_ref[0, pl.ds(pl.multiple_of(j * dk, dk), dk), :]
            return (lax.dot_general(kt, q1, nt, preferred_element_type=F32),
                    lax.dot_general(kt, q2, nt, preferred_element_type=F32))

        def bias_tiles(j):
            kb0 = j * (dk // tk) - (tq // tk) * qi
            return jnp.concatenate(
                [bias_ref[jnp.clip(kb0 + i, ATTN_OFF_LO, ATTN_OFF_HI) - ATTN_OFF_LO, 0]
                 for i in range(dk // tk)], axis=0)

        def far_bias(j):
            return jnp.where(j > qi, far_hi, far_lo)

        def accumulate(sh1, sh2):
            l1 = jnp.zeros((1, tq), F32)
            l2 = jnp.zeros((1, tq), F32)
            a1 = jnp.zeros((DIFF_V_DIM, tq), F32)
            a2 = jnp.zeros((DIFF_V_DIM, tq), F32)
            nxt = qk(group(order[0]))
            for idx, dj in enumerate(order):
                j = group(dj)
                s1, s2 = nxt
                if idx + 1 < n_grp:
                    nxt = qk(group(order[idx + 1]))
                if dj in near:
                    bias = bias_tiles(j)
                    p1 = jnp.exp2(s1 + bias - sh1)
                    p2 = jnp.exp2(s2 + bias - sh2)
                else:
                    c = far_bias(j)
                    p1 = jnp.exp2(s1 - (sh1 - c))
                    p2 = jnp.exp2(s2 - (sh2 - c))
                vt = vt_ref[:, pl.ds(pl.multiple_of(j * dk, dk), dk)]
                l1 = l1 + jnp.sum(p1, axis=0, keepdims=True)
                l2 = l2 + jnp.sum(p2, axis=0, keepdims=True)
                a1 = a1 + jnp.dot(vt, p1.astype(BF16), preferred_element_type=F32)
                a2 = a2 + jnp.dot(vt, p2.astype(BF16), preferred_element_type=F32)
            return l1, a1, l2, a2

        def exact_max():
            m1 = jnp.full((1, tq), NEG_BIG, F32)
            m2 = jnp.full((1, tq), NEG_BIG, F32)
            for dj in order:
                j = group(dj)
                s1, s2 = qk(j)
                if dj in near:
                    bias = bias_tiles(j)
                    b1 = jnp.max(s1 + bias, axis=0, keepdims=True)
                    b2 = jnp.max(s2 + bias, axis=0, keepdims=True)
                else:
                    c = far_bias(j)
                    b1 = jnp.max(s1, axis=0, keepdims=True) + c
                    b2 = jnp.max(s2, axis=0, keepdims=True) + c
                m1 = jnp.maximum(m1, b1)
                m2 = jnp.maximum(m2, b2)
            return m1, m2

        def finish(l1, a1, l2, a2):
            out = a1 * (1.0 / l1) - lam * (a2 * (1.0 / l2))
            out = out * lax.rsqrt(jnp.mean(out * out, axis=0, keepdims=True) + EPS)
            out = out * (g_ref[...] * (1.0 - lam_init))
            o_ref[0, pl.ds(q0, tq), :] = out.T.astype(o_ref.dtype)

        bound = jnp.sqrt(half_sq_norms(q) * kmax) * ATTN_BOUND_SLACK + bmax
        l1, a1, l2, a2 = accumulate(bound[0:1], bound[1:2])
        finish(l1, a1, l2, a2)
        ok = jnp.minimum(jnp.min(l1), jnp.min(l2)) > ATTN_MIN_DENOM
        return ok, lambda: finish(*accumulate(*exact_max()))

    def q_body(i, carry):
        blocks = [q_block(ATTN_Q_UNROLL * i + u) for u in range(ATTN_Q_UNROLL)]
        for ok, redo in blocks:
            pl.when(jnp.logical_not(ok))(redo)
        return carry

    lax.fori_loop(0, nq // ATTN_Q_UNROLL, q_body, 0)


def _diff_attn(u, v_t, table, bias_tiles, lam_params, subln_g, lam_init):
    b, s, _ = u.shape
    assert ATTN_DOT_KEYS == ATTN_TQ and ATTN_DOT_KEYS % ATTN_TK == 0
    assert s % ATTN_TQ == 0 and s // ATTN_DOT_KEYS >= 4
    hd2 = 2 * DIFF_HEAD_DIM
    k_blk0 = DIFF_QK_COLS // hd2
    return pl.pallas_call(
        functools.partial(_diff_attn_kernel, seq=s, lam_init=lam_init),
        grid=(b, DIFF_N_HEADS),
        in_specs=[
            pl.BlockSpec(memory_space=pltpu.SMEM),
            _resident((4, DIFF_HEAD_DIM)),
            pl.BlockSpec((1, s, hd2), lambda i, h: (i, 0, h)),
            pl.BlockSpec((1, s, hd2), lambda i, h: (i, 0, k_blk0 + h)),
            pl.BlockSpec((DIFF_V_DIM, s), lambda i, h: (h, i)),
            pl.BlockSpec((ATTN_BIAS_TILES, 1, ATTN_TK, ATTN_TQ), lambda i, h: (0, h, 0, 0)),
            _resident((DIFF_V_DIM, 1)),
        ],
        out_specs=pl.BlockSpec((1, s, DIFF_V_DIM), lambda i, h: (i, 0, h)),
        out_shape=jax.ShapeDtypeStruct((b, s, DIFF_WIDTH), BF16),
        compiler_params=_params(2),
        name="diff_attn",
    )(table, lam_params, u, u, v_t, bias_tiles, subln_g.reshape(DIFF_V_DIM, 1))


def _out_proj_kernel(mix_ref, q_ref, kv_ref, w_ref, x_ref, g_ref, o_ref, *, k_mix):
    q = q_ref[0]
    kv = kv_ref[0]
    nt = (((1,), (1,)), ((), ()))
    scale = X_HEAD_DIM ** -0.5
    heads = [slice(h * X_HEAD_DIM, (h + 1) * X_HEAD_DIM) for h in range(X_N_HEADS)]
    logits = [lax.dot_general(q[:, cs], kv[:, cs], nt, preferred_element_type=F32) * scale for cs in heads]
    o = jnp.dot(mix_ref[0], w_ref[0:k_mix, :], preferred_element_type=F32)
    mems = []
    for h, cs in enumerate(heads):
        vs = slice(X_WIDTH + cs.start, X_WIDTH + cs.stop)
        p = jnp.exp(logits[h] - jnp.max(logits[h], axis=-1, keepdims=True))
        l = jnp.sum(p, axis=-1, keepdims=True)
        mem = jnp.dot(p.astype(BF16), kv[:, vs], preferred_element_type=F32) * (1.0 / l)
        mems.append(mem.astype(BF16))
    o = o + jnp.dot(jnp.concatenate(mems, axis=1), w_ref[k_mix:k_mix + X_WIDTH, :],
                    preferred_element_type=F32)
    o_ref[0] = x_ref[0] + _rms(o) * g_ref[...]


def _out_proj(mix, u, q_blk, kv, w_out, x, g, name):
    b, s, k_mix = mix.shape
    d = x.shape[-1]
    return pl.pallas_call(
        functools.partial(_out_proj_kernel, k_mix=k_mix),
        grid=(b, s // TOKEN_TILE),
        in_specs=[
            pl.BlockSpec((1, TOKEN_TILE, k_mix), lambda i, j: (i, j, 0)),
            pl.BlockSpec((1, TOKEN_TILE, X_WIDTH), lambda i, j: (i, j, q_blk)),
            pl.BlockSpec((1, MEM_LEN, 2 * X_WIDTH), lambda i, j: (i, 0, 0)),
            _resident((k_mix + X_WIDTH, d)),
            pl.BlockSpec((1, TOKEN_TILE, d), lambda i, j: (i, j, 0)),
            _resident((1, d)),
        ],
        out_specs=pl.BlockSpec((1, TOKEN_TILE, d), lambda i, j: (i, j, 0)),
        out_shape=jax.ShapeDtypeStruct(x.shape, F32),
        compiler_params=_params(2),
        name=name,
    )(mix, u, kv, w_out, x, g.reshape(1, d))


def _mlp_kernel(x_ref, g1_ref, w1_ref, w2_ref, g2_ref, o_ref, *, ff_chunk):
    x = x_ref[...]
    h = (_rms(x) * g1_ref[...]).astype(BF16)
    f = None
    for c in range(0, D_FF, ff_chunk):
        a = jnp.dot(h, w1_ref[:, c:c + ff_chunk], preferred_element_type=F32)
        a = jnp.square(jnp.maximum(a, 0.0)).astype(BF16)
        part = jnp.dot(a, w2_ref[c:c + ff_chunk, :], preferred_element_type=F32)
        f = part if f is None else f + part
    o_ref[...] = x + _rms(f) * g2_ref[...]


def _mlp(x2d, g1, w1, w2, g2):
    m, d = x2d.shape
    return pl.pallas_call(
        functools.partial(_mlp_kernel, ff_chunk=1024),
        grid=(m // TOKEN_TILE,),
        in_specs=[
            pl.BlockSpec((TOKEN_TILE, d), lambda i: (i, 0)),
            _resident((1, d)),
            _resident((d, D_FF)),
            _resident((D_FF, d)),
            _resident((1, d)),
        ],
        out_specs=pl.BlockSpec((TOKEN_TILE, d), lambda i: (i, 0)),
        out_shape=jax.ShapeDtypeStruct((m, d), F32),
        compiler_params=_params(1),
        name="mlp",
    )(x2d, g1.reshape(1, d), w1, w2, g2.reshape(1, d))


def _lambda_init(layer_idx):
    return 0.8 - 0.6 * math.exp(-0.3 * layer_idx)


def _group_dt_layout(a, b, s):
    a = a.reshape(b, 2, 2, SSD_N_GROUPS, SSD_HEADS_PER_GROUP, s)
    return jnp.transpose(a, (0, 3, 1, 2, 4, 5)).reshape(b, SSD_N_GROUPS, 16, s)


def _trunk(x, mem, p):
    b, s, d = x.shape
    tokens = b * s

    def kv_proj(i):
        kv = _norm_proj(mem.reshape(b * MEM_LEN, d), p["x_mem_norm"][i], p["x_w_kv"][i],
                        [(2 * X_WIDTH, BF16)], name=f"kv_proj{i}")[0]
        return kv.reshape(b, MEM_LEN, 2 * X_WIDTH)

    def mlp(x, i):
        return _mlp(x.reshape(tokens, d), p["norm_pre_mlp"][i], p["mlp_w1"][i], p["mlp_w2"][i],
                    p["norm_post_mlp"][i]).reshape(b, s, d)

    u_main, dt_raw = _norm_proj(
        x.reshape(tokens, d), p["norm_pre_mix"][0], p["ssd_w_in"],
        [(SSD_MAIN_COLS, BF16), (SSD_DT_PAD, F32)], name="ssd_in_proj")
    u_main = u_main.reshape(b, s, SSD_MAIN_COLS)
    dt_t = jnp.transpose(dt_raw.reshape(b, s, SSD_DT_PAD)[:, :, :SSD_DT_COLS], (0, 2, 1))
    col_arr, row_arr = _dt_prep(dt_t, p["ssd_dt_bias"], p["ssd_a_log"])
    a_col = jnp.transpose(_group_dt_layout(col_arr, b, s), (0, 1, 3, 2))
    a_row = _group_dt_layout(row_arr, b, s)
    mix = _ssd(u_main, a_col, a_row, p["ssd_conv_w"], p["ssd_conv_b"], p["ssd_d_e"], p["ssd_norm"])
    x = _out_proj(mix, u_main, SSD_MAIN_COLS // X_WIDTH - 1, kv_proj(0), p["ssd_w_out"], x,
                  p["norm_post_mix"][0], name="ssd_out_proj")
    x = mlp(x, 0)

    u, v_t = _norm_proj(x.reshape(tokens, d), p["norm_pre_mix"][1], p["diff_w_in"],
                        [(p["diff_w_in"].shape[1], BF16)], name="diff_in_proj", w_t=p["diff_w_v_t"])
    u = u.reshape(b, s, -1)
    mix = _diff_attn(u, v_t, p["rel_bias_table"], p["bias_tiles"], p["diff_lambda"], p["diff_subln"],
                     _lambda_init(1))
    x = _out_proj(mix, u, u.shape[-1] // X_WIDTH - 1, kv_proj(1), p["diff_w_out"], x,
                  p["norm_post_mix"][1], name="diff_out_proj")
    return mlp(x, 1)


def kernel(x_prompt, x_sample, mem_prompt, mem_sample, rel_bias_table, norm_pre_mix, norm_post_mix,
           norm_pre_mlp, norm_post_mlp, ssd_w_in, ssd_conv_w, ssd_conv_b, ssd_dt_bias, ssd_a_log, ssd_d,
           ssd_norm, ssd_w_out, diff_w_in, diff_lambda, diff_subln, diff_w_out, x_mem_norm, x_w_kv,
           mlp_w1, mlp_w2):
    w_in = ssd_w_in[0]
    mix_cols = SSD_D_INNER + SSD_XBC
    w_in = jnp.concatenate([
        w_in[:, :mix_cols], w_in[:, mix_cols + SSD_DT_COLS:], w_in[:, mix_cols:mix_cols + SSD_DT_COLS],
        jnp.zeros((D_MODEL, SSD_DT_PAD - SSD_DT_COLS), w_in.dtype)], axis=1)
    p = {
        "rel_bias_table": rel_bias_table,
        "bias_tiles": _bias_tiles(rel_bias_table),
        "norm_pre_mix": norm_pre_mix, "norm_post_mix": norm_post_mix,
        "norm_pre_mlp": norm_pre_mlp, "norm_post_mlp": norm_post_mlp,
        "ssd_w_in": w_in.astype(BF16),
        "ssd_conv_w": ssd_conv_w[0], "ssd_conv_b": ssd_conv_b[0].reshape(1, SSD_XBC),
        "ssd_dt_bias": ssd_dt_bias[0], "ssd_a_log": ssd_a_log[0],
        "ssd_d_e": jnp.repeat(ssd_d[0], SSD_HEAD_DIM).reshape(1, SSD_D_INNER),
        "ssd_norm": ssd_norm[0].reshape(1, SSD_D_INNER),
        "ssd_w_out": ssd_w_out[0].astype(BF16),
        "diff_w_in": jnp.concatenate([
            diff_w_in[0][:, :DIFF_QK_COLS] * (DIFF_HEAD_DIM ** -0.5 * LOG2E),
            diff_w_in[0][:, DIFF_QK_COLS:2 * DIFF_QK_COLS],
            diff_w_in[0][:, 2 * DIFF_QK_COLS + DIFF_WIDTH:]], axis=1).astype(BF16),
        "diff_w_v_t": diff_w_in[0][:, 2 * DIFF_QK_COLS:2 * DIFF_QK_COLS + DIFF_WIDTH].T.astype(BF16),
        "diff_lambda": diff_lambda[0], "diff_subln": diff_subln[0],
        "diff_w_out": diff_w_out[0].astype(BF16),
        "x_mem_norm": x_mem_norm, "x_w_kv": x_w_kv.astype(BF16),
        "mlp_w1": mlp_w1.astype(BF16), "mlp_w2": mlp_w2.astype(BF16),
    }
    return (_trunk(x_prompt, mem_prompt, p), _trunk(x_sample, mem_sample, p))
```

```python
import functools
import math

import jax
import jax.numpy as jnp
from jax import lax
from jax.experimental import pallas as pl
from jax.experimental.pallas import tpu as pltpu

F32 = jnp.float32
BF16 = jnp.bfloat16

EPS = 1e-6
D_MODEL = 1024
MEM_LEN = 256

SSD_D_INNER = 2048
SSD_HEAD_DIM = 64
SSD_N_HEADS = 32
SSD_N_GROUPS = 8
SSD_HEADS_PER_GROUP = SSD_N_HEADS // SSD_N_GROUPS
SSD_D_STATE = 128
SSD_BC = SSD_N_GROUPS * SSD_D_STATE
SSD_XBC = SSD_D_INNER + 2 * SSD_BC
SSD_CONV_WIDTH = 5
SSD_CHUNK = 128
SSD_GROUP_WIDTH = SSD_D_INNER // SSD_N_GROUPS
SSD_MAIN_COLS = SSD_D_INNER + SSD_XBC + 1024
SSD_DT_COLS = 2 * SSD_N_HEADS
SSD_DT_PAD = 128
SSD_OUT_UNROLL = 8

DIFF_N_HEADS = 8
DIFF_HEAD_DIM = 64
DIFF_V_DIM = 128
DIFF_QK_COLS = 1024
DIFF_WIDTH = 1024
REL_BUCKETS = 32

X_N_HEADS = 4
X_HEAD_DIM = 256
X_WIDTH = 1024
D_FF = 4096

VMEM_LIMIT_BYTES = 56 * 1024 * 1024

TOKEN_TILE = 512
ATTN_TQ = 512
ATTN_TK = 256
ATTN_DOT_KEYS = 512
ATTN_Q_UNROLL = 2
ATTN_Q_FULL_UNROLL_MAX = 4
ATTN_OFF_LO = -2
ATTN_OFF_HI = ATTN_TQ // ATTN_TK + 1
ATTN_BIAS_TILES = ATTN_OFF_HI - ATTN_OFF_LO + 1
ATTN_BOUND_SLACK = 1.02
ATTN_MIN_DENOM = 2.0 ** -60
NEG_BIG = -1e30
LOG2E = math.log2(math.e)


def _params(n_grid_dims):
    return pltpu.CompilerParams(
        dimension_semantics=("arbitrary",) * n_grid_dims,
        vmem_limit_bytes=VMEM_LIMIT_BYTES,
    )


def _rms(x):
    return x * lax.rsqrt(jnp.mean(x * x, axis=-1, keepdims=True) + EPS)


def _silu(x):
    h = 0.5 * x
    return h + h * jnp.tanh(h)


def _resident(shape):
    nd = len(shape)
    return pl.BlockSpec(shape, lambda *_: (0,) * nd, pipeline_mode=pl.Buffered(1))


def _norm_proj_kernel(x_ref, g_ref, w_ref, *rest, col_chunk, has_t):
    h = (_rms(x_ref[...]) * g_ref[...]).astype(BF16)
    o_refs = rest[1:] if has_t else rest
    if has_t:
        wt_ref, ot_ref = rest[0], rest[-1]
        o_refs = o_refs[:-1]
        ot_ref[...] = lax.dot_general(wt_ref[...], h, (((1,), (1,)), ((), ())),
                                      preferred_element_type=F32).astype(ot_ref.dtype)
    c0 = 0
    for o_ref in o_refs:
        n = o_ref.shape[-1]
        for j in range(0, n, col_chunk):
            cw = min(col_chunk, n - j)
            o_ref[:, j:j + cw] = jnp.dot(
                h, w_ref[:, c0 + j:c0 + j + cw], preferred_element_type=F32
            ).astype(o_ref.dtype)
        c0 += n


def _norm_proj(x2d, g, w, outs, name, w_t=None):
    m, d = x2d.shape
    n = w.shape[1]
    assert sum(wd for wd, _ in outs) == n and m % TOKEN_TILE == 0
    in_specs = [
        pl.BlockSpec((TOKEN_TILE, d), lambda i: (i, 0)),
        _resident((1, d)),
        _resident((d, n)),
    ]
    out_specs = [pl.BlockSpec((TOKEN_TILE, wd), lambda i: (i, 0)) for wd, _ in outs]
    out_shape = [jax.ShapeDtypeStruct((m, wd), dt) for wd, dt in outs]
    args = [x2d, g.reshape(1, d), w]
    if w_t is not None:
        in_specs.append(_resident(w_t.shape))
        out_specs.append(pl.BlockSpec((w_t.shape[0], TOKEN_TILE), lambda i: (0, i)))
        out_shape.append(jax.ShapeDtypeStruct((w_t.shape[0], m), BF16))
        args.append(w_t)
    return pl.pallas_call(
        functools.partial(_norm_proj_kernel, col_chunk=1024, has_t=w_t is not None),
        grid=(m // TOKEN_TILE,),
        in_specs=in_specs,
        out_specs=out_specs,
        out_shape=out_shape,
        compiler_params=_params(1),
        name=name,
    )(*args)


def _dt_prep_kernel(dt_ref, bias_ref, alog_ref, col_ref, row_ref, *, seq):
    n = SSD_DT_COLS
    nh = SSD_N_HEADS
    row = lax.broadcasted_iota(jnp.int32, (SSD_CHUNK, SSD_CHUNK), 0)
    col = lax.broadcasted_iota(jnp.int32, (SSD_CHUNK, SSD_CHUNK), 1)
    prefix = (row <= col).astype(F32)
    suffix = (row >= col).astype(F32)
    is_fwd = lax.broadcasted_iota(jnp.int32, (n, SSD_CHUNK), 0) < nh
    a = -jnp.exp(alog_ref[...])
    for c in range(seq // SSD_CHUNK):
        sl = slice(c * SSD_CHUNK, (c + 1) * SSD_CHUNK)
        dt = jax.nn.softplus(dt_ref[0, :, sl] + bias_ref[...])
        da = dt * a
        cum_f = jnp.dot(da, prefix, preferred_element_type=F32, precision=lax.Precision.HIGHEST)
        cum_b = jnp.dot(da, suffix, preferred_element_type=F32, precision=lax.Precision.HIGHEST)
        cum = jnp.where(is_fwd, cum_f, cum_b)
        col_ref[0, 0:n, sl] = cum
        col_ref[0, n:2 * n, sl] = dt
        row_ref[0, 0:n, sl] = cum - jnp.log(dt)
        row_ref[0, n:n + nh, sl] = jnp.log(dt[0:nh] + dt[nh:n])
        row_ref[0, n + nh:2 * n, sl] = jnp.zeros((nh, SSD_CHUNK), F32)


def _dt_prep(dt_t, dt_bias, a_log):
    b, n, s = dt_t.shape
    return pl.pallas_call(
        functools.partial(_dt_prep_kernel, seq=s),
        grid=(b,),
        in_specs=[
            pl.BlockSpec((1, n, s), lambda i: (i, 0, 0)),
            _resident((n, 1)),
            _resident((n, 1)),
        ],
        out_specs=[pl.BlockSpec((1, 2 * n, s), lambda i: (i, 0, 0))] * 2,
        out_shape=[jax.ShapeDtypeStruct((b, 2 * n, s), F32)] * 2,
        compiler_params=_params(1),
        name="ssd_dt_prep",
    )(dt_t, dt_bias.reshape(n, 1), a_log.reshape(n, 1))


def _ssd_kernel(z_ref, x_ref, b_ref, c_ref, acol_ref, arow_ref,
                cwx_ref, cwb_ref, cwc_ref, cbx_ref, cbb_ref, cbc_ref, dskip_ref, ng_ref,
                o_ref, dec_scr, cs_scr, y_scr, hf_scr, hb_scr, *, seq):
    L = SSD_CHUNK
    R = SSD_HEADS_PER_GROUP
    GW = SSD_GROUP_WIDTH
    NS = SSD_D_STATE
    KW = SSD_CONV_WIDTH
    nc = seq // L
    halo = 16
    nt = (((1,), (1,)), ((), ()))

    lane_head = lax.broadcasted_iota(jnp.int32, (1, GW), 1) // SSD_HEAD_DIM
    row_i = lax.broadcasted_iota(jnp.int32, (L, L), 0)
    col_i = lax.broadcasted_iota(jnp.int32, (L, L), 1)
    e_row = lax.broadcasted_iota(jnp.int32, (3 * 2 * R, 2 * GW), 0)
    e_col = lax.broadcasted_iota(jnp.int32, (3 * 2 * R, 2 * GW), 1)
    expand_mat = jnp.where(e_row % (2 * R) == e_col // SSD_HEAD_DIM, 1.0, 0.0).astype(BF16)

    def expand(cols):
        hi = cols.astype(BF16)
        r1 = cols - hi.astype(F32)
        mid = r1.astype(BF16)
        lo = (r1 - mid.astype(F32)).astype(BF16)
        return jnp.dot(jnp.concatenate([hi, mid, lo], axis=1), expand_mat, preferred_element_type=F32)

    mid_tap = KW // 2
    seg_rows = [L + halo] * mid_tap + [L] + [L + halo] * mid_tap
    seg_off = [halo] * mid_tap + [0] * (mid_tap + 1)
    s_row = lax.broadcasted_iota(jnp.int32, (L, sum(seg_rows)), 0)
    s_col = lax.broadcasted_iota(jnp.int32, (L, sum(seg_rows)), 1)
    hit = jnp.zeros((L, sum(seg_rows)), jnp.bool_)
    k0 = 0
    for w in range(KW):
        hit = hit | (s_col == k0 + seg_off[w] + s_row + w - mid_tap)
        k0 += seg_rows[w]
    shift_mat = jnp.where(hit, 1.0, 0.0).astype(BF16)

    def conv_stage(c):
        c0 = pl.multiple_of(c * L, L)
        p0 = pl.multiple_of(jnp.maximum(c0 - halo, 0), halo)
        n0 = pl.multiple_of(jnp.minimum(c0 + L, seq - halo), halo)

        def rows(r0, n):
            return jnp.concatenate([ref[0, pl.ds(r0, n), :] for ref in (x_ref, b_ref, c_ref)], axis=1)

        main = rows(c0, L)
        prev = rows(p0, halo)
        nxt = rows(n0, halo)
        prev = jnp.where(c > 0, prev, jnp.zeros_like(prev))
        nxt = jnp.where(c < nc - 1, nxt, jnp.zeros_like(nxt))
        before = jnp.concatenate([prev, main], axis=0)
        after = jnp.concatenate([main, nxt], axis=0)
        cw = jnp.concatenate([cwx_ref[...], cwb_ref[...], cwc_ref[...]], axis=1).astype(BF16)
        cb = jnp.concatenate([cbx_ref[...], cbb_ref[...], cbc_ref[...]], axis=1)
        segs = [before] * mid_tap + [main] + [after] * mid_tap
        scaled = jnp.concatenate([segs[w] * cw[w:w + 1, :] for w in range(KW)], axis=0)
        act = _silu(jnp.dot(shift_mat, scaled, preferred_element_type=F32) + cb)
        bcf = act[:, GW:GW + NS]
        return act[:, :GW], bcf.astype(BF16), act[:, GW + NS:].astype(BF16), bcf.T.astype(BF16)

    def decay_stage(c, vals):
        xc, bc, cc, bct = vals
        c0 = pl.multiple_of(c * L, L)
        g = lax.dot_general(cc, bc, nt, preferred_element_type=F32)
        acol = acol_ref[0, 0, pl.ds(c0, L), :]
        arow = arow_ref[0, 0, :, pl.ds(c0, L)]
        ws = []
        for r in range(R):
            arg = jnp.where(row_i > col_i,
                            acol[:, r:r + 1] - arow[r:r + 1, :],
                            acol[:, R + r:R + r + 1] - arow[R + r:R + r + 1, :])
            arg = jnp.where(row_i == col_i, arow[2 * R + r:2 * R + r + 1, :], arg)
            ws.append((g * jnp.exp(arg)).astype(BF16))
        return xc, cc, bct, jnp.concatenate(ws, axis=1)

    def local_stage(c, vals):
        xc, cc, bct, wcat = vals
        c0 = pl.multiple_of(c * L, L)
        cs_scr[pl.ds(c0, L), :] = cc
        acol = acol_ref[0, 0, pl.ds(c0, L), :]
        cum = acol[:, 0:2 * R]
        lane8 = lax.broadcasted_iota(jnp.int32, (1, 2 * R), 1)
        tot = jnp.where(lane8 < R, cum[L - 1:L, :], cum[0:1, :])
        ss = expand(jnp.concatenate([jnp.exp(tot - cum) * acol[:, 2 * R:4 * R],
                                     jnp.broadcast_to(jnp.exp(tot), (8, 2 * R))], axis=0))
        xb = xc.astype(BF16)
        xbd = jnp.concatenate([jnp.where(lane_head == r, xb, jnp.zeros_like(xb)) for r in range(R)], axis=0)
        y = jnp.dot(wcat, xbd, preferred_element_type=F32)
        y_scr[pl.ds(c0, L), :] = y + xc * dskip_ref[...]
        dec_scr[c] = ss[L:, :]
        xsc = (jnp.concatenate([xc, xc], axis=1) * ss[:L, :]).astype(BF16)
        s_loc = jnp.dot(bct, xsc, preferred_element_type=F32)
        hf_scr[c] = s_loc[:, :GW]
        hb_scr[c] = s_loc[:, GW:]

    def conv_pair(p):
        return conv_stage(2 * p), conv_stage(2 * p + 1)

    def decay_pair(p, vals):
        return decay_stage(2 * p, vals[0]), decay_stage(2 * p + 1, vals[1])

    def local_pair(p, vals):
        local_stage(2 * p, vals[0])
        local_stage(2 * p + 1, vals[1])

    def local_body(p, carry):
        dec_vals, conv_vals = carry
        nxt_conv = conv_pair(p + 2)
        nxt_dec = decay_pair(p + 1, conv_vals)
        local_pair(p, dec_vals)
        return nxt_dec, nxt_conv

    npair = nc // 2
    carry = (decay_pair(0, conv_pair(0)), conv_pair(1))
    def local_body2(i, carry):
        return local_body(2 * i + 1, local_body(2 * i, carry))

    dec_vals, conv_vals = lax.fori_loop(0, (npair - 2) // 2, local_body2, carry)
    last = decay_pair(npair - 1, conv_vals)
    local_pair(npair - 2, dec_vals)
    local_pair(npair - 1, last)

    def state_body(i, carry):
        sf, sb = carry
        cb = nc - 1 - i
        s_loc = hf_scr[i]
        hf_scr[i] = sf
        sf = sf * dec_scr[i][0:1, :GW] + s_loc
        s_loc = hb_scr[cb]
        hb_scr[cb] = sb
        sb = sb * dec_scr[cb][0:1, GW:] + s_loc
        return sf, sb

    zero = jnp.zeros((NS, GW), F32)
    lax.fori_loop(0, nc, state_body, (zero, zero))

    def out_chunk(c):
        c0 = pl.multiple_of(c * L, L)
        cc = cs_scr[pl.ds(c0, L), :]
        h = jnp.concatenate([hf_scr[c], hb_scr[c]], axis=1).astype(BF16)
        yi = jnp.dot(cc, h, preferred_element_type=F32)
        yi = yi * expand(jnp.exp(acol_ref[0, 0, pl.ds(c0, L), 0:2 * R]))
        y = y_scr[pl.ds(c0, L), :] + yi[:, :GW] + yi[:, GW:]
        y = y * _silu(z_ref[0, pl.ds(c0, L), :].astype(F32))
        o_ref[0, pl.ds(c0, L), :] = (_rms(y) * ng_ref[...]).astype(o_ref.dtype)

    def out_body(i, carry):
        for j in range(SSD_OUT_UNROLL):
            out_chunk(SSD_OUT_UNROLL * i + j)
        return carry

    lax.fori_loop(0, nc // SSD_OUT_UNROLL, out_body, 0)


def _ssd(u_main, a_col, a_row, conv_w, conv_b, d_skip_e, norm_g):
    b, s, _ = u_main.shape
    gw, ds = SSD_GROUP_WIDTH, SSD_D_STATE
    nc = s // SSD_CHUNK
    assert nc % SSD_OUT_UNROLL == 0 and nc % 4 == 0 and nc >= 8
    x_blk0 = SSD_D_INNER // gw
    b_blk0 = (2 * SSD_D_INNER) // ds
    c_blk0 = (2 * SSD_D_INNER + SSD_BC) // ds
    cw_b0 = SSD_D_INNER // ds
    cw_c0 = (SSD_D_INNER + SSD_BC) // ds
    kw = SSD_CONV_WIDTH
    return pl.pallas_call(
        functools.partial(_ssd_kernel, seq=s),
        grid=(b, SSD_N_GROUPS),
        in_specs=[
            pl.BlockSpec((1, s, gw), lambda i, g: (i, 0, g)),
            pl.BlockSpec((1, s, gw), lambda i, g: (i, 0, x_blk0 + g)),
            pl.BlockSpec((1, s, ds), lambda i, g: (i, 0, b_blk0 + g)),
            pl.BlockSpec((1, s, ds), lambda i, g: (i, 0, c_blk0 + g)),
            pl.BlockSpec((1, 1, s, 16), lambda i, g: (i, g, 0, 0)),
            pl.BlockSpec((1, 1, 16, s), lambda i, g: (i, g, 0, 0)),
            pl.BlockSpec((kw, gw), lambda i, g: (0, g)),
            pl.BlockSpec((kw, ds), lambda i, g: (0, cw_b0 + g)),
            pl.BlockSpec((kw, ds), lambda i, g: (0, cw_c0 + g)),
            pl.BlockSpec((1, gw), lambda i, g: (0, g)),
            pl.BlockSpec((1, ds), lambda i, g: (0, cw_b0 + g)),
            pl.BlockSpec((1, ds), lambda i, g: (0, cw_c0 + g)),
            pl.BlockSpec((1, gw), lambda i, g: (0, g)),
            pl.BlockSpec((1, gw), lambda i, g: (0, g)),
        ],
        out_specs=pl.BlockSpec((1, s, gw), lambda i, g: (i, 0, g)),
        out_shape=jax.ShapeDtypeStruct((b, s, SSD_D_INNER), BF16),
        scratch_shapes=[
            pltpu.VMEM((nc, 8, 2 * gw), F32),
            pltpu.VMEM((s, ds), BF16),
            pltpu.VMEM((s, gw), F32),
            pltpu.VMEM((nc, ds, gw), F32),
            pltpu.VMEM((nc, ds, gw), F32),
        ],
        compiler_params=_params(2),
        name="ssd_scan",
    )(u_main, u_main, u_main, u_main, a_col, a_row,
      conv_w, conv_w, conv_w, conv_b, conv_b, conv_b, d_skip_e, norm_g)


_BUCKET_STEPS = (12, 16, 23, 32, 46, 64, 91)


def _bias_tile_kernel(table_ref, o_ref):
    d = pl.program_id(0) + ATTN_OFF_LO
    k = lax.broadcasted_iota(jnp.int32, (ATTN_TK, ATTN_TQ), 0)
    q = lax.broadcasted_iota(jnp.int32, (ATTN_TK, ATTN_TQ), 1)
    rel = k + d * ATTN_TK - q
    n = jnp.abs(rel)
    large = jnp.full_like(n, 8)
    for step in _BUCKET_STEPS:
        large = large + (n >= step).astype(jnp.int32)
    bucket = jnp.where(rel > 0, 16, 0) + jnp.where(n < 8, n, large)
    for h in range(DIFF_N_HEADS):
        acc = jnp.zeros((ATTN_TK, ATTN_TQ), F32)
        for bkt in range(REL_BUCKETS):
            acc = jnp.where(bucket == bkt, table_ref[bkt, h] * LOG2E, acc)
        o_ref[0, h] = acc


def _bias_tiles(table):
    return pl.pallas_call(
        _bias_tile_kernel,
        grid=(ATTN_BIAS_TILES,),
        in_specs=[pl.BlockSpec(memory_space=pltpu.SMEM)],
        out_specs=pl.BlockSpec((1, DIFF_N_HEADS, ATTN_TK, ATTN_TQ), lambda i: (i, 0, 0, 0)),
        out_shape=jax.ShapeDtypeStruct((ATTN_BIAS_TILES, DIFF_N_HEADS, ATTN_TK, ATTN_TQ), F32),
        compiler_params=_params(1),
        name="rel_bias_tiles",
    )(table)


def _diff_attn_kernel(table_ref, lam_ref, q_ref, k_ref, vt_ref, bias_ref, g_ref, o_ref,
                      *, seq, lam_init):
    tq, tk, dk = ATTN_TQ, ATTN_TK, ATTN_DOT_KEYS
    nq = seq // tq
    h = pl.program_id(1)
    nt = (((1,), (1,)), ((), ()))
    hd = DIFF_HEAD_DIM
    lane = lax.broadcasted_iota(jnp.int32, (1, 2 * hd), 1)
    sel_row = lax.broadcasted_iota(jnp.int32, (8, 2 * hd), 0)
    sel_lane = lax.broadcasted_iota(jnp.int32, (8, 2 * hd), 1)
    sel = jnp.where(sel_row == sel_lane // hd, 1.0, 0.0).astype(BF16)

    def half_sq_norms(x):
        xf = x.astype(F32)
        return lax.dot_general(sel, (xf * xf).astype(BF16), nt, preferred_element_type=F32)

    kmax = half_sq_norms(k_ref[0, 0:dk, :])
    for c in range(1, seq // dk):
        kmax = jnp.maximum(kmax, half_sq_norms(k_ref[0, c * dk:(c + 1) * dk, :]))
    kmax = jnp.max(kmax, axis=1, keepdims=True)

    bmax = table_ref[0, h]
    for bkt in range(1, REL_BUCKETS):
        bmax = jnp.maximum(bmax, table_ref[bkt, h])
    bmax = bmax * LOG2E
    far_lo = table_ref[REL_BUCKETS // 2 - 1, h] * LOG2E
    far_hi = table_ref[REL_BUCKETS - 1, h] * LOG2E
    n_grp = seq // dk

    lp = lam_ref[...]
    lam = (jnp.exp(jnp.sum(lp[0:1] * lp[1:2], axis=-1, keepdims=True))
           - jnp.exp(jnp.sum(lp[2:3] * lp[3:4], axis=-1, keepdims=True)) + lam_init)

    def q_block(qi):
        q0 = pl.multiple_of(qi * tq, tq)
        q = q_ref[0, pl.ds(q0, tq), :]
        zero = jnp.zeros_like(q)
        q1 = jnp.where(lane < hd, q, zero)
        q2 = jnp.where(lane >= hd, q, zero)

        near = (0, 1, -1)
        order = list(near) + list(range(2, n_grp - 1))

        def group(dj):
            return lax.rem(qi + dj + n_grp, n_grp)

        def qk(j):
            kt = k_ref[0, pl.ds(pl.multiple_of(j * dk, dk), dk), :]
            return (lax.dot_general(kt, q1, nt, preferred_element_type=F32),
                    lax.dot_general(kt, q2, nt, preferred_element_type=F32))

        def bias_tiles(j):
            kb0 = j * (dk // tk) - (tq // tk) * qi
            return jnp.concatenate(
                [bias_ref[jnp.clip(kb0 + i, ATTN_OFF_LO, ATTN_OFF_HI) - ATTN_OFF_LO, 0]
                 for i in range(dk // tk)], axis=0)

        def far_bias(j):
            return jnp.where(j > qi, far_hi, far_lo)

        def accumulate(sh1, sh2):
            l1 = jnp.zeros((1, tq), F32)
            l2 = jnp.zeros((1, tq), F32)
            a1 = jnp.zeros((DIFF_V_DIM, tq), F32)
            a2 = jnp.zeros((DIFF_V_DIM, tq), F32)
            nxt = qk(group(order[0]))
            for idx, dj in enumerate(order):
                j = group(dj)
                s1, s2 = nxt
                if idx + 1 < n_grp:
                    nxt = qk(group(order[idx + 1]))
                if dj in near:
                    bias = bias_tiles(j)
                    p1 = jnp.exp2(s1 + bias - sh1)
                    p2 = jnp.exp2(s2 + bias - sh2)
                else:
                    c = far_bias(j)
                    p1 = jnp.exp2(s1 - (sh1 - c))
                    p2 = jnp.exp2(s2 - (sh2 - c))
                vt = vt_ref[:, pl.ds(pl.multiple_of(j * dk, dk), dk)]
                l1 = l1 + jnp.sum(p1, axis=0, keepdims=True)
                l2 = l2 + jnp.sum(p2, axis=0, keepdims=True)
                a1 = a1 + jnp.dot(vt, p1.astype(BF16), preferred_element_type=F32)
                a2 = a2 + jnp.dot(vt, p2.astype(BF16), preferred_element_type=F32)
            return l1, a1, l2, a2

        def exact_max():
            m1 = jnp.full((1, tq), NEG_BIG, F32)
            m2 = jnp.full((1, tq), NEG_BIG, F32)
            for dj in order:
                j = group(dj)
                s1, s2 = qk(j)
                if dj in near:
                    bias = bias_tiles(j)
                    b1 = jnp.max(s1 + bias, axis=0, keepdims=True)
                    b2 = jnp.max(s2 + bias, axis=0, keepdims=True)
                else:
                    c = far_bias(j)
                    b1 = jnp.max(s1, axis=0, keepdims=True) + c
                    b2 = jnp.max(s2, axis=0, keepdims=True) + c
                m1 = jnp.maximum(m1, b1)
                m2 = jnp.maximum(m2, b2)
            return m1, m2

        def finish(l1, a1, l2, a2):
            out = a1 * (1.0 / l1) - lam * (a2 * (1.0 / l2))
            out = out * lax.rsqrt(jnp.mean(out * out, axis=0, keepdims=True) + EPS)
            out = out * (g_ref[...] * (1.0 - lam_init))
            o_ref[0, pl.ds(q0, tq), :] = out.T.astype(o_ref.dtype)

        bound = jnp.sqrt(half_sq_norms(q) * kmax) * ATTN_BOUND_SLACK + bmax
        l1, a1, l2, a2 = accumulate(bound[0:1], bound[1:2])
        finish(l1, a1, l2, a2)
        ok = jnp.minimum(jnp.min(l1), jnp.min(l2)) > ATTN_MIN_DENOM
        return ok, lambda: finish(*accumulate(*exact_max()))

    unroll = nq if nq <= ATTN_Q_FULL_UNROLL_MAX else ATTN_Q_UNROLL

    def q_body(i, carry):
        blocks = [q_block(unroll * i + u) for u in range(unroll)]
        for ok, redo in blocks:
            pl.when(jnp.logical_not(ok))(redo)
        return carry

    lax.fori_loop(0, nq // unroll, q_body, 0)


def _diff_attn(u, v_t, table, bias_tiles, lam_params, subln_g, lam_init):
    b, s, _ = u.shape
    assert ATTN_DOT_KEYS == ATTN_TQ and ATTN_DOT_KEYS % ATTN_TK == 0
    assert s % ATTN_TQ == 0 and s // ATTN_DOT_KEYS >= 4
    hd2 = 2 * DIFF_HEAD_DIM
    k_blk0 = DIFF_QK_COLS // hd2
    return pl.pallas_call(
        functools.partial(_diff_attn_kernel, seq=s, lam_init=lam_init),
        grid=(b, DIFF_N_HEADS),
        in_specs=[
            pl.BlockSpec(memory_space=pltpu.SMEM),
            _resident((4, DIFF_HEAD_DIM)),
            pl.BlockSpec((1, s, hd2), lambda i, h: (i, 0, h)),
            pl.BlockSpec((1, s, hd2), lambda i, h: (i, 0, k_blk0 + h)),
            pl.BlockSpec((DIFF_V_DIM, s), lambda i, h: (h, i)),
            pl.BlockSpec((ATTN_BIAS_TILES, 1, ATTN_TK, ATTN_TQ), lambda i, h: (0, h, 0, 0)),
            _resident((DIFF_V_DIM, 1)),
        ],
        out_specs=pl.BlockSpec((1, s, DIFF_V_DIM), lambda i, h: (i, 0, h)),
        out_shape=jax.ShapeDtypeStruct((b, s, DIFF_WIDTH), BF16),
        compiler_params=_params(2),
        name="diff_attn",
    )(table, lam_params, u, u, v_t, bias_tiles, subln_g.reshape(DIFF_V_DIM, 1))


def _layer_tail_kernel(mix_ref, q_ref, kv_ref, w_ref, x_ref, g_ref, g1_ref, w1_ref, w2_ref, g2_ref,
                       o_ref, *, k_mix, ff_chunk):
    q = q_ref[0]
    kv = kv_ref[0]
    nt = (((1,), (1,)), ((), ()))
    scale = X_HEAD_DIM ** -0.5
    heads = [slice(h * X_HEAD_DIM, (h + 1) * X_HEAD_DIM) for h in range(X_N_HEADS)]
    logits = [lax.dot_general(q[:, cs], kv[:, cs], nt, preferred_element_type=F32) * scale for cs in heads]
    o = jnp.dot(mix_ref[0], w_ref[0:k_mix, :], preferred_element_type=F32)
    mems = []
    for h, cs in enumerate(heads):
        vs = slice(X_WIDTH + cs.start, X_WIDTH + cs.stop)
        p = jnp.exp(logits[h] - jnp.max(logits[h], axis=-1, keepdims=True))
        l = jnp.sum(p, axis=-1, keepdims=True)
        mem = jnp.dot(p.astype(BF16), kv[:, vs], preferred_element_type=F32) * (1.0 / l)
        mems.append(mem.astype(BF16))
    o = o + jnp.dot(jnp.concatenate(mems, axis=1), w_ref[k_mix:k_mix + X_WIDTH, :],
                    preferred_element_type=F32)
    x = x_ref[0] + _rms(o) * g_ref[...]

    h = (_rms(x) * g1_ref[...]).astype(BF16)
    f = None
    for c in range(0, D_FF, ff_chunk):
        a = jnp.dot(h, w1_ref[:, c:c + ff_chunk], preferred_element_type=F32)
        a = jnp.square(jnp.maximum(a, 0.0)).astype(BF16)
        part = jnp.dot(a, w2_ref[c:c + ff_chunk, :], preferred_element_type=F32)
        f = part if f is None else f + part
    o_ref[0] = x + _rms(f) * g2_ref[...]


def _layer_tail(mix, u, q_blk, kv, w_out, x, g, g1, w1, w2, g2, name):
    b, s, k_mix = mix.shape
    d = x.shape[-1]
    return pl.pallas_call(
        functools.partial(_layer_tail_kernel, k_mix=k_mix, ff_chunk=1024),
        grid=(b, s // TOKEN_TILE),
        in_specs=[
            pl.BlockSpec((1, TOKEN_TILE, k_mix), lambda i, j: (i, j, 0)),
            pl.BlockSpec((1, TOKEN_TILE, X_WIDTH), lambda i, j: (i, j, q_blk)),
            pl.BlockSpec((1, MEM_LEN, 2 * X_WIDTH), lambda i, j: (i, 0, 0)),
            _resident((k_mix + X_WIDTH, d)),
            pl.BlockSpec((1, TOKEN_TILE, d), lambda i, j: (i, j, 0)),
            _resident((1, d)),
            _resident((1, d)),
            _resident((d, D_FF)),
            _resident((D_FF, d)),
            _resident((1, d)),
        ],
        out_specs=pl.BlockSpec((1, TOKEN_TILE, d), lambda i, j: (i, j, 0)),
        out_shape=jax.ShapeDtypeStruct(x.shape, F32),
        compiler_params=_params(2),
        name=name,
    )(mix, u, kv, w_out, x, g.reshape(1, d), g1.reshape(1, d), w1, w2, g2.reshape(1, d))


def _lambda_init(layer_idx):
    return 0.8 - 0.6 * math.exp(-0.3 * layer_idx)


def _group_dt_layout(a, b, s):
    a = a.reshape(b, 2, 2, SSD_N_GROUPS, SSD_HEADS_PER_GROUP, s)
    return jnp.transpose(a, (0, 3, 1, 2, 4, 5)).reshape(b, SSD_N_GROUPS, 16, s)


def _trunk(x, mem, p):
    b, s, d = x.shape
    tokens = b * s

    def kv_proj(i):
        kv = _norm_proj(mem.reshape(b * MEM_LEN, d), p["x_mem_norm"][i], p["x_w_kv"][i],
                        [(2 * X_WIDTH, BF16)], name=f"kv_proj{i}")[0]
        return kv.reshape(b, MEM_LEN, 2 * X_WIDTH)

    def layer_tail(i, mix, u, w_out, x, name):
        return _layer_tail(mix, u, u.shape[-1] // X_WIDTH - 1, kv_proj(i), w_out, x,
                           p["norm_post_mix"][i], p["norm_pre_mlp"][i], p["mlp_w1"][i], p["mlp_w2"][i],
                           p["norm_post_mlp"][i], name=name)

    u_main, dt_raw = _norm_proj(
        x.reshape(tokens, d), p["norm_pre_mix"][0], p["ssd_w_in"],
        [(SSD_MAIN_COLS, BF16), (SSD_DT_PAD, F32)], name="ssd_in_proj")
    u_main = u_main.reshape(b, s, SSD_MAIN_COLS)
    dt_t = jnp.transpose(dt_raw.reshape(b, s, SSD_DT_PAD)[:, :, :SSD_DT_COLS], (0, 2, 1))
    col_arr, row_arr = _dt_prep(dt_t, p["ssd_dt_bias"], p["ssd_a_log"])
    a_col = jnp.transpose(_group_dt_layout(col_arr, b, s), (0, 1, 3, 2))
    a_row = _group_dt_layout(row_arr, b, s)
    mix = _ssd(u_main, a_col, a_row, p["ssd_conv_w"], p["ssd_conv_b"], p["ssd_d_e"], p["ssd_norm"])
    x = layer_tail(0, mix, u_main, p["ssd_w_out"], x, "ssd_layer_tail")

    u, v_t = _norm_proj(x.reshape(tokens, d), p["norm_pre_mix"][1], p["diff_w_in"],
                        [(p["diff_w_in"].shape[1], BF16)], name="diff_in_proj", w_t=p["diff_w_v_t"])
    u = u.reshape(b, s, -1)
    mix = _diff_attn(u, v_t, p["rel_bias_table"], p["bias_tiles"], p["diff_lambda"], p["diff_subln"],
                     _lambda_init(1))
    return layer_tail(1, mix, u, p["diff_w_out"], x, "diff_layer_tail")


def kernel(x_prompt, x_sample, mem_prompt, mem_sample, rel_bias_table, norm_pre_mix, norm_post_mix,
           norm_pre_mlp, norm_post_mlp, ssd_w_in, ssd_conv_w, ssd_conv_b, ssd_dt_bias, ssd_a_log, ssd_d,
           ssd_norm, ssd_w_out, diff_w_in, diff_lambda, diff_subln, diff_w_out, x_mem_norm, x_w_kv,
           mlp_w1, mlp_w2):
    w_in = ssd_w_in[0]
    mix_cols = SSD_D_INNER + SSD_XBC
    w_in = jnp.concatenate([
        w_in[:, :mix_cols], w_in[:, mix_cols + SSD_DT_COLS:], w_in[:, mix_cols:mix_cols + SSD_DT_COLS],
        jnp.zeros((D_MODEL, SSD_DT_PAD - SSD_DT_COLS), w_in.dtype)], axis=1)
    p = {
        "rel_bias_table": rel_bias_table,
        "bias_tiles": _bias_tiles(rel_bias_table),
        "norm_pre_mix": norm_pre_mix, "norm_post_mix": norm_post_mix,
        "norm_pre_mlp": norm_pre_mlp, "norm_post_mlp": norm_post_mlp,
        "ssd_w_in": w_in.astype(BF16),
        "ssd_conv_w": ssd_conv_w[0], "ssd_conv_b": ssd_conv_b[0].reshape(1, SSD_XBC),
        "ssd_dt_bias": ssd_dt_bias[0], "ssd_a_log": ssd_a_log[0],
        "ssd_d_e": jnp.repeat(ssd_d[0], SSD_HEAD_DIM).reshape(1, SSD_D_INNER),
        "ssd_norm": ssd_norm[0].reshape(1, SSD_D_INNER),
        "ssd_w_out": ssd_w_out[0].astype(BF16),
        "diff_w_in": jnp.concatenate([
            diff_w_in[0][:, :DIFF_QK_COLS] * (DIFF_HEAD_DIM ** -0.5 * LOG2E),
            diff_w_in[0][:, DIFF_QK_COLS:2 * DIFF_QK_COLS],
            diff_w_in[0][:, 2 * DIFF_QK_COLS + DIFF_WIDTH:]], axis=1).astype(BF16),
        "diff_w_v_t": diff_w_in[0][:, 2 * DIFF_QK_COLS:2 * DIFF_QK_COLS + DIFF_WIDTH].T.astype(BF16),
        "diff_lambda": diff_lambda[0], "diff_subln": diff_subln[0],
        "diff_w_out": diff_w_out[0].astype(BF16),
        "x_mem_norm": x_mem_norm, "x_w_kv": x_w_kv.astype(BF16),
        "mlp_w1": mlp_w1.astype(BF16), "mlp_w2": mlp_w2.astype(BF16),
    }
    return (_trunk(x_prompt, mem_prompt, p), _trunk(x_sample, mem_sample, p))
```

```python
import functools
import math

import jax
import jax.numpy as jnp
from jax import lax
from jax.experimental import pallas as pl
from jax.experimental.pallas import tpu as pltpu

F32 = jnp.float32
BF16 = jnp.bfloat16

EPS = 1e-6
D_MODEL = 1024
MEM_LEN = 256

SSD_D_INNER = 2048
SSD_HEAD_DIM = 64
SSD_N_HEADS = 32
SSD_N_GROUPS = 8
SSD_HEADS_PER_GROUP = SSD_N_HEADS // SSD_N_GROUPS
SSD_D_STATE = 128
SSD_BC = SSD_N_GROUPS * SSD_D_STATE
SSD_XBC = SSD_D_INNER + 2 * SSD_BC
SSD_CONV_WIDTH = 5
SSD_CHUNK = 128
SSD_GROUP_WIDTH = SSD_D_INNER // SSD_N_GROUPS
SSD_MAIN_COLS = SSD_D_INNER + SSD_XBC + 1024
SSD_DT_COLS = 2 * SSD_N_HEADS
SSD_DT_PAD = 128
SSD_OUT_UNROLL = 8
SSD_LOCAL_ITERS = 2

DIFF_N_HEADS = 8
DIFF_HEAD_DIM = 64
DIFF_V_DIM = 128
DIFF_QK_COLS = 1024
DIFF_WIDTH = 1024
REL_BUCKETS = 32

X_N_HEADS = 4
X_HEAD_DIM = 256
X_WIDTH = 1024
D_FF = 4096

VMEM_LIMIT_BYTES = 56 * 1024 * 1024

TOKEN_TILE = 512
ATTN_TQ = 512
ATTN_TK = 256
ATTN_DOT_KEYS = 512
ATTN_Q_UNROLL = 4
ATTN_OFF_LO = -2
ATTN_OFF_HI = ATTN_TQ // ATTN_TK + 1
ATTN_BIAS_TILES = ATTN_OFF_HI - ATTN_OFF_LO + 1
ATTN_BOUND_SLACK = 1.02
ATTN_MIN_DENOM = 2.0 ** -60
NEG_BIG = -1e30
LOG2E = math.log2(math.e)


def _params(n_grid_dims):
    return pltpu.CompilerParams(
        dimension_semantics=("arbitrary",) * n_grid_dims,
        vmem_limit_bytes=VMEM_LIMIT_BYTES,
    )


def _rms(x):
    return x * lax.rsqrt(jnp.mean(x * x, axis=-1, keepdims=True) + EPS)


def _silu(x):
    h = 0.5 * x
    return h + h * jnp.tanh(h)


def _resident(shape):
    nd = len(shape)
    return pl.BlockSpec(shape, lambda *_: (0,) * nd, pipeline_mode=pl.Buffered(1))


def _norm_proj_kernel(x_ref, g_ref, w_ref, *rest, col_chunk, has_t):
    h = (_rms(x_ref[...]) * g_ref[...]).astype(BF16)
    o_refs = rest[1:] if has_t else rest
    if has_t:
        wt_ref, ot_ref = rest[0], rest[-1]
        o_refs = o_refs[:-1]
        ot_ref[...] = lax.dot_general(wt_ref[...], h, (((1,), (1,)), ((), ())),
                                      preferred_element_type=F32).astype(ot_ref.dtype)
    c0 = 0
    for o_ref in o_refs:
        n = o_ref.shape[-1]
        for j in range(0, n, col_chunk):
            cw = min(col_chunk, n - j)
            o_ref[:, j:j + cw] = jnp.dot(
                h, w_ref[:, c0 + j:c0 + j + cw], preferred_element_type=F32
            ).astype(o_ref.dtype)
        c0 += n


def _norm_proj(x2d, g, w, outs, name, w_t=None):
    m, d = x2d.shape
    n = w.shape[1]
    assert sum(wd for wd, _ in outs) == n and m % TOKEN_TILE == 0
    in_specs = [
        pl.BlockSpec((TOKEN_TILE, d), lambda i: (i, 0)),
        _resident((1, d)),
        _resident((d, n)),
    ]
    out_specs = [pl.BlockSpec((TOKEN_TILE, wd), lambda i: (i, 0)) for wd, _ in outs]
    out_shape = [jax.ShapeDtypeStruct((m, wd), dt) for wd, dt in outs]
    args = [x2d, g.reshape(1, d), w]
    if w_t is not None:
        in_specs.append(_resident(w_t.shape))
        out_specs.append(pl.BlockSpec((w_t.shape[0], TOKEN_TILE), lambda i: (0, i)))
        out_shape.append(jax.ShapeDtypeStruct((w_t.shape[0], m), BF16))
        args.append(w_t)
    return pl.pallas_call(
        functools.partial(_norm_proj_kernel, col_chunk=1024, has_t=w_t is not None),
        grid=(m // TOKEN_TILE,),
        in_specs=in_specs,
        out_specs=out_specs,
        out_shape=out_shape,
        compiler_params=_params(1),
        name=name,
    )(*args)


def _dt_prep_kernel(dt_ref, bias_ref, alog_ref, col_ref, row_ref, *, seq):
    n = SSD_DT_COLS
    nh = SSD_N_HEADS
    row = lax.broadcasted_iota(jnp.int32, (SSD_CHUNK, SSD_CHUNK), 0)
    col = lax.broadcasted_iota(jnp.int32, (SSD_CHUNK, SSD_CHUNK), 1)
    prefix = (row <= col).astype(F32)
    suffix = (row >= col).astype(F32)
    is_fwd = lax.broadcasted_iota(jnp.int32, (n, SSD_CHUNK), 0) < nh
    a = -jnp.exp(alog_ref[...])
    for c in range(seq // SSD_CHUNK):
        sl = slice(c * SSD_CHUNK, (c + 1) * SSD_CHUNK)
        dt = jax.nn.softplus(dt_ref[0, :, sl] + bias_ref[...])
        da = dt * a
        cum_f = jnp.dot(da, prefix, preferred_element_type=F32, precision=lax.Precision.HIGHEST)
        cum_b = jnp.dot(da, suffix, preferred_element_type=F32, precision=lax.Precision.HIGHEST)
        cum = jnp.where(is_fwd, cum_f, cum_b)
        col_ref[0, 0:n, sl] = cum
        col_ref[0, n:2 * n, sl] = dt
        row_ref[0, 0:n, sl] = cum - jnp.log(dt)
        row_ref[0, n:n + nh, sl] = jnp.log(dt[0:nh] + dt[nh:n])
        row_ref[0, n + nh:2 * n, sl] = jnp.zeros((nh, SSD_CHUNK), F32)


def _dt_prep(dt_t, dt_bias, a_log):
    b, n, s = dt_t.shape
    return pl.pallas_call(
        functools.partial(_dt_prep_kernel, seq=s),
        grid=(b,),
        in_specs=[
            pl.BlockSpec((1, n, s), lambda i: (i, 0, 0)),
            _resident((n, 1)),
            _resident((n, 1)),
        ],
        out_specs=[pl.BlockSpec((1, 2 * n, s), lambda i: (i, 0, 0))] * 2,
        out_shape=[jax.ShapeDtypeStruct((b, 2 * n, s), F32)] * 2,
        compiler_params=_params(1),
        name="ssd_dt_prep",
    )(dt_t, dt_bias.reshape(n, 1), a_log.reshape(n, 1))


def _ssd_kernel(z_ref, x_ref, b_ref, c_ref, acol_ref, arow_ref,
                cwx_ref, cwb_ref, cwc_ref, cbx_ref, cbb_ref, cbc_ref, dskip_ref, ng_ref,
                o_ref, dec_scr, cs_scr, y_scr, hf_scr, hb_scr, *, seq):
    L = SSD_CHUNK
    R = SSD_HEADS_PER_GROUP
    GW = SSD_GROUP_WIDTH
    NS = SSD_D_STATE
    KW = SSD_CONV_WIDTH
    nc = seq // L
    halo = 16
    nt = (((1,), (1,)), ((), ()))

    lane_head = lax.broadcasted_iota(jnp.int32, (1, GW), 1) // SSD_HEAD_DIM
    row_i = lax.broadcasted_iota(jnp.int32, (L, L), 0)
    col_i = lax.broadcasted_iota(jnp.int32, (L, L), 1)
    e_row = lax.broadcasted_iota(jnp.int32, (3 * 2 * R, 2 * GW), 0)
    e_col = lax.broadcasted_iota(jnp.int32, (3 * 2 * R, 2 * GW), 1)
    expand_mat = jnp.where(e_row % (2 * R) == e_col // SSD_HEAD_DIM, 1.0, 0.0).astype(BF16)

    def expand(cols):
        hi = cols.astype(BF16)
        r1 = cols - hi.astype(F32)
        mid = r1.astype(BF16)
        lo = (r1 - mid.astype(F32)).astype(BF16)
        return jnp.dot(jnp.concatenate([hi, mid, lo], axis=1), expand_mat, preferred_element_type=F32)

    mid_tap = KW // 2
    seg_rows = [L + halo] * mid_tap + [L] + [L + halo] * mid_tap
    seg_off = [halo] * mid_tap + [0] * (mid_tap + 1)
    s_row = lax.broadcasted_iota(jnp.int32, (L, sum(seg_rows)), 0)
    s_col = lax.broadcasted_iota(jnp.int32, (L, sum(seg_rows)), 1)
    hit = jnp.zeros((L, sum(seg_rows)), jnp.bool_)
    k0 = 0
    for w in range(KW):
        hit = hit | (s_col == k0 + seg_off[w] + s_row + w - mid_tap)
        k0 += seg_rows[w]
    shift_mat = jnp.where(hit, 1.0, 0.0).astype(BF16)

    def conv_stage(c):
        c0 = pl.multiple_of(c * L, L)
        p0 = pl.multiple_of(jnp.maximum(c0 - halo, 0), halo)
        n0 = pl.multiple_of(jnp.minimum(c0 + L, seq - halo), halo)

        def rows(r0, n):
            return jnp.concatenate([ref[0, pl.ds(r0, n), :] for ref in (x_ref, b_ref, c_ref)], axis=1)

        main = rows(c0, L)
        prev = rows(p0, halo)
        nxt = rows(n0, halo)
        prev = jnp.where(c > 0, prev, jnp.zeros_like(prev))
        nxt = jnp.where(c < nc - 1, nxt, jnp.zeros_like(nxt))
        before = jnp.concatenate([prev, main], axis=0)
        after = jnp.concatenate([main, nxt], axis=0)
        cw = jnp.concatenate([cwx_ref[...], cwb_ref[...], cwc_ref[...]], axis=1).astype(BF16)
        cb = jnp.concatenate([cbx_ref[...], cbb_ref[...], cbc_ref[...]], axis=1)
        segs = [before] * mid_tap + [main] + [after] * mid_tap
        scaled = jnp.concatenate([segs[w] * cw[w:w + 1, :] for w in range(KW)], axis=0)
        act = _silu(jnp.dot(shift_mat, scaled, preferred_element_type=F32) + cb)
        bcf = act[:, GW:GW + NS]
        return act[:, :GW], bcf.astype(BF16), act[:, GW + NS:].astype(BF16), bcf.T.astype(BF16)

    def decay_stage(c, vals):
        xc, bc, cc, bct = vals
        c0 = pl.multiple_of(c * L, L)
        g = lax.dot_general(cc, bc, nt, preferred_element_type=F32)
        acol = acol_ref[0, 0, pl.ds(c0, L), :]
        arow = arow_ref[0, 0, :, pl.ds(c0, L)]
        ws = []
        for r in range(R):
            arg = jnp.where(row_i > col_i,
                            acol[:, r:r + 1] - arow[r:r + 1, :],
                            acol[:, R + r:R + r + 1] - arow[R + r:R + r + 1, :])
            arg = jnp.where(row_i == col_i, arow[2 * R + r:2 * R + r + 1, :], arg)
            ws.append((g * jnp.exp(arg)).astype(BF16))
        return xc, cc, bct, jnp.concatenate(ws, axis=1)

    def local_stage(c, vals):
        xc, cc, bct, wcat = vals
        c0 = pl.multiple_of(c * L, L)
        cs_scr[pl.ds(c0, L), :] = cc
        acol = acol_ref[0, 0, pl.ds(c0, L), :]
        cum = acol[:, 0:2 * R]
        lane8 = lax.broadcasted_iota(jnp.int32, (1, 2 * R), 1)
        tot = jnp.where(lane8 < R, cum[L - 1:L, :], cum[0:1, :])
        ss = expand(jnp.concatenate([jnp.exp(tot - cum) * acol[:, 2 * R:4 * R],
                                     jnp.broadcast_to(jnp.exp(tot), (8, 2 * R))], axis=0))
        xb = xc.astype(BF16)
        xbd = jnp.concatenate([jnp.where(lane_head == r, xb, jnp.zeros_like(xb)) for r in range(R)], axis=0)
        y = jnp.dot(wcat, xbd, preferred_element_type=F32)
        y_scr[pl.ds(c0, L), :] = y + xc * dskip_ref[...]
        dec_scr[c] = ss[L:, :]
        xsc = (jnp.concatenate([xc, xc], axis=1) * ss[:L, :]).astype(BF16)
        s_loc = jnp.dot(bct, xsc, preferred_element_type=F32)
        hf_scr[c] = s_loc[:, :GW]
        hb_scr[c] = s_loc[:, GW:]

    def conv_pair(p):
        return conv_stage(2 * p), conv_stage(2 * p + 1)

    def decay_pair(p, vals):
        return decay_stage(2 * p, vals[0]), decay_stage(2 * p + 1, vals[1])

    def local_pair(p, vals):
        local_stage(2 * p, vals[0])
        local_stage(2 * p + 1, vals[1])

    def local_body(p, carry):
        dec_vals, conv_vals = carry
        nxt_conv = conv_pair(p + 2)
        nxt_dec = decay_pair(p + 1, conv_vals)
        local_pair(p, dec_vals)
        return nxt_dec, nxt_conv

    npair = nc // 2
    carry = (decay_pair(0, conv_pair(0)), conv_pair(1))
    steps = (npair - 2) // SSD_LOCAL_ITERS

    def local_multi(i, carry):
        for k in range(steps):
            carry = local_body(steps * i + k, carry)
        return carry

    dec_vals, conv_vals = lax.fori_loop(0, SSD_LOCAL_ITERS, local_multi, carry)
    last = decay_pair(npair - 1, conv_vals)
    local_pair(npair - 2, dec_vals)
    local_pair(npair - 1, last)

    def state_body(i, carry):
        sf, sb = carry
        cb = nc - 1 - i
        s_loc = hf_scr[i]
        hf_scr[i] = sf
        sf = sf * dec_scr[i][0:1, :GW] + s_loc
        s_loc = hb_scr[cb]
        hb_scr[cb] = sb
        sb = sb * dec_scr[cb][0:1, GW:] + s_loc
        return sf, sb

    zero = jnp.zeros((NS, GW), F32)
    lax.fori_loop(0, nc, state_body, (zero, zero))

    def out_chunk(c):
        c0 = pl.multiple_of(c * L, L)
        cc = cs_scr[pl.ds(c0, L), :]
        h = jnp.concatenate([hf_scr[c], hb_scr[c]], axis=1).astype(BF16)
        yi = jnp.dot(cc, h, preferred_element_type=F32)
        yi = yi * expand(jnp.exp(acol_ref[0, 0, pl.ds(c0, L), 0:2 * R]))
        y = y_scr[pl.ds(c0, L), :] + yi[:, :GW] + yi[:, GW:]
        y = y * _silu(z_ref[0, pl.ds(c0, L), :].astype(F32))
        o_ref[0, pl.ds(c0, L), :] = (_rms(y) * ng_ref[...]).astype(o_ref.dtype)

    def out_body(i, carry):
        for j in range(SSD_OUT_UNROLL):
            out_chunk(SSD_OUT_UNROLL * i + j)
        return carry

    lax.fori_loop(0, nc // SSD_OUT_UNROLL, out_body, 0)


def _ssd(u_main, a_col, a_row, conv_w, conv_b, d_skip_e, norm_g):
    b, s, _ = u_main.shape
    gw, ds = SSD_GROUP_WIDTH, SSD_D_STATE
    nc = s // SSD_CHUNK
    assert nc % SSD_OUT_UNROLL == 0 and nc % 2 == 0 and (nc // 2 - 2) % SSD_LOCAL_ITERS == 0 and nc >= 8
    x_blk0 = SSD_D_INNER // gw
    b_blk0 = (2 * SSD_D_INNER) // ds
    c_blk0 = (2 * SSD_D_INNER + SSD_BC) // ds
    cw_b0 = SSD_D_INNER // ds
    cw_c0 = (SSD_D_INNER + SSD_BC) // ds
    kw = SSD_CONV_WIDTH
    return pl.pallas_call(
        functools.partial(_ssd_kernel, seq=s),
        grid=(b, SSD_N_GROUPS),
        in_specs=[
            pl.BlockSpec((1, s, gw), lambda i, g: (i, 0, g)),
            pl.BlockSpec((1, s, gw), lambda i, g: (i, 0, x_blk0 + g)),
            pl.BlockSpec((1, s, ds), lambda i, g: (i, 0, b_blk0 + g)),
            pl.BlockSpec((1, s, ds), lambda i, g: (i, 0, c_blk0 + g)),
            pl.BlockSpec((1, 1, s, 16), lambda i, g: (i, g, 0, 0)),
            pl.BlockSpec((1, 1, 16, s), lambda i, g: (i, g, 0, 0)),
            pl.BlockSpec((kw, gw), lambda i, g: (0, g)),
            pl.BlockSpec((kw, ds), lambda i, g: (0, cw_b0 + g)),
            pl.BlockSpec((kw, ds), lambda i, g: (0, cw_c0 + g)),
            pl.BlockSpec((1, gw), lambda i, g: (0, g)),
            pl.BlockSpec((1, ds), lambda i, g: (0, cw_b0 + g)),
            pl.BlockSpec((1, ds), lambda i, g: (0, cw_c0 + g)),
            pl.BlockSpec((1, gw), lambda i, g: (0, g)),
            pl.BlockSpec((1, gw), lambda i, g: (0, g)),
        ],
        out_specs=pl.BlockSpec((1, s, gw), lambda i, g: (i, 0, g)),
        out_shape=jax.ShapeDtypeStruct((b, s, SSD_D_INNER), BF16),
        scratch_shapes=[
            pltpu.VMEM((nc, 8, 2 * gw), F32),
            pltpu.VMEM((s, ds), BF16),
            pltpu.VMEM((s, gw), F32),
            pltpu.VMEM((nc, ds, gw), F32),
            pltpu.VMEM((nc, ds, gw), F32),
        ],
        compiler_params=_params(2),
        name="ssd_scan",
    )(u_main, u_main, u_main, u_main, a_col, a_row,
      conv_w, conv_w, conv_w, conv_b, conv_b, conv_b, d_skip_e, norm_g)


_BUCKET_STEPS = (12, 16, 23, 32, 46, 64, 91)


def _bias_tile_kernel(table_ref, o_ref):
    d = pl.program_id(0) + ATTN_OFF_LO
    k = lax.broadcasted_iota(jnp.int32, (ATTN_TK, ATTN_TQ), 0)
    q = lax.broadcasted_iota(jnp.int32, (ATTN_TK, ATTN_TQ), 1)
    rel = k + d * ATTN_TK - q
    n = jnp.abs(rel)
    large = jnp.full_like(n, 8)
    for step in _BUCKET_STEPS:
        large = large + (n >= step).astype(jnp.int32)
    bucket = jnp.where(rel > 0, 16, 0) + jnp.where(n < 8, n, large)
    for h in range(DIFF_N_HEADS):
        acc = jnp.zeros((ATTN_TK, ATTN_TQ), F32)
        for bkt in range(REL_BUCKETS):
            acc = jnp.where(bucket == bkt, table_ref[bkt, h] * LOG2E, acc)
        o_ref[0, h] = acc


def _bias_tiles(table):
    return pl.pallas_call(
        _bias_tile_kernel,
        grid=(ATTN_BIAS_TILES,),
        in_specs=[pl.BlockSpec(memory_space=pltpu.SMEM)],
        out_specs=pl.BlockSpec((1, DIFF_N_HEADS, ATTN_TK, ATTN_TQ), lambda i: (i, 0, 0, 0)),
        out_shape=jax.ShapeDtypeStruct((ATTN_BIAS_TILES, DIFF_N_HEADS, ATTN_TK, ATTN_TQ), F32),
        compiler_params=_params(1),
        name="rel_bias_tiles",
    )(table)


def _diff_attn_kernel(table_ref, lam_ref, q_ref, k_ref, vt_ref, bias_ref, g_ref, o_ref,
                      *, seq, lam_init):
    tq, tk, dk = ATTN_TQ, ATTN_TK, ATTN_DOT_KEYS
    nq = seq // tq
    h = pl.program_id(1)
    nt = (((1,), (1,)), ((), ()))
    hd = DIFF_HEAD_DIM
    lane = lax.broadcasted_iota(jnp.int32, (1, 2 * hd), 1)
    sel_row = lax.broadcasted_iota(jnp.int32, (8, 2 * hd), 0)
    sel_lane = lax.broadcasted_iota(jnp.int32, (8, 2 * hd), 1)
    sel = jnp.where(sel_row == sel_lane // hd, 1.0, 0.0).astype(BF16)

    def half_sq_norms(x):
        xf = x.astype(F32)
        return lax.dot_general(sel, (xf * xf).astype(BF16), nt, preferred_element_type=F32)

    kmax = half_sq_norms(k_ref[0, 0:dk, :])
    for c in range(1, seq // dk):
        kmax = jnp.maximum(kmax, half_sq_norms(k_ref[0, c * dk:(c + 1) * dk, :]))
    kmax = jnp.max(kmax, axis=1, keepdims=True)

    bmax = table_ref[0, h]
    for bkt in range(1, REL_BUCKETS):
        bmax = jnp.maximum(bmax, table_ref[bkt, h])
    bmax = bmax * LOG2E
    far_lo = table_ref[REL_BUCKETS // 2 - 1, h] * LOG2E
    far_hi = table_ref[REL_BUCKETS - 1, h] * LOG2E
    n_grp = seq // dk

    lp = lam_ref[...]
    lam = (jnp.exp(jnp.sum(lp[0:1] * lp[1:2], axis=-1, keepdims=True))
           - jnp.exp(jnp.sum(lp[2:3] * lp[3:4], axis=-1, keepdims=True)) + lam_init)

    def q_block(qi):
        q0 = pl.multiple_of(qi * tq, tq)
        q = q_ref[0, pl.ds(q0, tq), :]
        zero = jnp.zeros_like(q)
        q1 = jnp.where(lane < hd, q, zero)
        q2 = jnp.where(lane >= hd, q, zero)

        near = (0, 1, -1)
        order = list(near) + list(range(2, n_grp - 1))

        def group(dj):
            return lax.rem(qi + dj + n_grp, n_grp)

        def qk(j):
            kt = k_ref[0, pl.ds(pl.multiple_of(j * dk, dk), dk), :]
            return (lax.dot_general(kt, q1, nt, preferred_element_type=F32),
                    lax.dot_general(kt, q2, nt, preferred_element_type=F32))

        def bias_tiles(j):
            kb0 = j * (dk // tk) - (tq // tk) * qi
            return jnp.concatenate(
                [bias_ref[jnp.clip(kb0 + i, ATTN_OFF_LO, ATTN_OFF_HI) - ATTN_OFF_LO, 0]
                 for i in range(dk // tk)], axis=0)

        def far_bias(j):
            return jnp.where(j > qi, far_hi, far_lo)

        def accumulate(sh1, sh2):
            l1 = jnp.zeros((1, tq), F32)
            l2 = jnp.zeros((1, tq), F32)
            a1 = jnp.zeros((DIFF_V_DIM, tq), F32)
            a2 = jnp.zeros((DIFF_V_DIM, tq), F32)
            nxt = qk(group(order[0]))
            for idx, dj in enumerate(order):
                j = group(dj)
                s1, s2 = nxt
                if idx + 1 < n_grp:
                    nxt = qk(group(order[idx + 1]))
                if dj in near:
                    bias = bias_tiles(j)
                    p1 = jnp.exp2(s1 + bias - sh1)
                    p2 = jnp.exp2(s2 + bias - sh2)
                else:
                    c = far_bias(j)
                    p1 = jnp.exp2(s1 - (sh1 - c))
                    p2 = jnp.exp2(s2 - (sh2 - c))
                vt = vt_ref[:, pl.ds(pl.multiple_of(j * dk, dk), dk)]
                l1 = l1 + jnp.sum(p1, axis=0, keepdims=True)
                l2 = l2 + jnp.sum(p2, axis=0, keepdims=True)
                a1 = a1 + jnp.dot(vt, p1.astype(BF16), preferred_element_type=F32)
                a2 = a2 + jnp.dot(vt, p2.astype(BF16), preferred_element_type=F32)
            return l1, a1, l2, a2

        def exact_max():
            m1 = jnp.full((1, tq), NEG_BIG, F32)
            m2 = jnp.full((1, tq), NEG_BIG, F32)
            for dj in order:
                j = group(dj)
                s1, s2 = qk(j)
                if dj in near:
                    bias = bias_tiles(j)
                    b1 = jnp.max(s1 + bias, axis=0, keepdims=True)
                    b2 = jnp.max(s2 + bias, axis=0, keepdims=True)
                else:
                    c = far_bias(j)
                    b1 = jnp.max(s1, axis=0, keepdims=True) + c
                    b2 = jnp.max(s2, axis=0, keepdims=True) + c
                m1 = jnp.maximum(m1, b1)
                m2 = jnp.maximum(m2, b2)
            return m1, m2

        def finish(l1, a1, l2, a2):
            out = a1 * (1.0 / l1) - lam * (a2 * (1.0 / l2))
            out = out * lax.rsqrt(jnp.mean(out * out, axis=0, keepdims=True) + EPS)
            out = out * (g_ref[...] * (1.0 - lam_init))
            o_ref[0, pl.ds(q0, tq), :] = out.T.astype(o_ref.dtype)

        bound = jnp.sqrt(half_sq_norms(q) * kmax) * ATTN_BOUND_SLACK + bmax
        l1, a1, l2, a2 = accumulate(bound[0:1], bound[1:2])
        finish(l1, a1, l2, a2)
        ok = jnp.minimum(jnp.min(l1), jnp.min(l2)) > ATTN_MIN_DENOM
        return ok, lambda: finish(*accumulate(*exact_max()))

    unroll = ATTN_Q_UNROLL

    def q_body(i, carry):
        blocks = [q_block(unroll * i + u) for u in range(unroll)]
        for ok, redo in blocks:
            pl.when(jnp.logical_not(ok))(redo)
        return carry

    lax.fori_loop(0, nq // unroll, q_body, 0)


def _diff_attn(u, v_t, table, bias_tiles, lam_params, subln_g, lam_init):
    b, s, _ = u.shape
    assert ATTN_DOT_KEYS == ATTN_TQ and ATTN_DOT_KEYS % ATTN_TK == 0
    assert s % (ATTN_TQ * ATTN_Q_UNROLL) == 0 and s // ATTN_DOT_KEYS >= 4
    hd2 = 2 * DIFF_HEAD_DIM
    k_blk0 = DIFF_QK_COLS // hd2
    return pl.pallas_call(
        functools.partial(_diff_attn_kernel, seq=s, lam_init=lam_init),
        grid=(b, DIFF_N_HEADS),
        in_specs=[
            pl.BlockSpec(memory_space=pltpu.SMEM),
            _resident((4, DIFF_HEAD_DIM)),
            pl.BlockSpec((1, s, hd2), lambda i, h: (i, 0, h)),
            pl.BlockSpec((1, s, hd2), lambda i, h: (i, 0, k_blk0 + h)),
            pl.BlockSpec((DIFF_V_DIM, s), lambda i, h: (h, i)),
            pl.BlockSpec((ATTN_BIAS_TILES, 1, ATTN_TK, ATTN_TQ), lambda i, h: (0, h, 0, 0)),
            _resident((DIFF_V_DIM, 1)),
        ],
        out_specs=pl.BlockSpec((1, s, DIFF_V_DIM), lambda i, h: (i, 0, h)),
        out_shape=jax.ShapeDtypeStruct((b, s, DIFF_WIDTH), BF16),
        compiler_params=_params(2),
        name="diff_attn",
    )(table, lam_params, u, u, v_t, bias_tiles, subln_g.reshape(DIFF_V_DIM, 1))


def _layer_tail_kernel(mix_ref, q_ref, kv_ref, w_ref, x_ref, g_ref, g1_ref, w1_ref, w2_ref, g2_ref,
                       o_ref, *, k_mix, ff_chunk):
    q = q_ref[0]
    kv = kv_ref[0]
    nt = (((1,), (1,)), ((), ()))
    scale = X_HEAD_DIM ** -0.5
    heads = [slice(h * X_HEAD_DIM, (h + 1) * X_HEAD_DIM) for h in range(X_N_HEADS)]
    logits = [lax.dot_general(q[:, cs], kv[:, cs], nt, preferred_element_type=F32) * scale for cs in heads]
    o = jnp.dot(mix_ref[0], w_ref[0:k_mix, :], preferred_element_type=F32)
    mems = []
    for h, cs in enumerate(heads):
        vs = slice(X_WIDTH + cs.start, X_WIDTH + cs.stop)
        p = jnp.exp(logits[h] - jnp.max(logits[h], axis=-1, keepdims=True))
        l = jnp.sum(p, axis=-1, keepdims=True)
        mem = jnp.dot(p.astype(BF16), kv[:, vs], preferred_element_type=F32) * (1.0 / l)
        mems.append(mem.astype(BF16))
    o = o + jnp.dot(jnp.concatenate(mems, axis=1), w_ref[k_mix:k_mix + X_WIDTH, :],
                    preferred_element_type=F32)
    x = x_ref[0] + _rms(o) * g_ref[...]

    h = (_rms(x) * g1_ref[...]).astype(BF16)
    f = None
    for c in range(0, D_FF, ff_chunk):
        a = jnp.dot(h, w1_ref[:, c:c + ff_chunk], preferred_element_type=F32)
        a = jnp.square(jnp.maximum(a, 0.0)).astype(BF16)
        part = jnp.dot(a, w2_ref[c:c + ff_chunk, :], preferred_element_type=F32)
        f = part if f is None else f + part
    o_ref[0] = x + _rms(f) * g2_ref[...]


def _layer_tail(mix, u, q_blk, kv, w_out, x, g, g1, w1, w2, g2, name):
    b, s, k_mix = mix.shape
    d = x.shape[-1]
    return pl.pallas_call(
        functools.partial(_layer_tail_kernel, k_mix=k_mix, ff_chunk=1024),
        grid=(b, s // TOKEN_TILE),
        in_specs=[
            pl.BlockSpec((1, TOKEN_TILE, k_mix), lambda i, j: (i, j, 0)),
            pl.BlockSpec((1, TOKEN_TILE, X_WIDTH), lambda i, j: (i, j, q_blk)),
            pl.BlockSpec((1, MEM_LEN, 2 * X_WIDTH), lambda i, j: (i, 0, 0)),
            _resident((k_mix + X_WIDTH, d)),
            pl.BlockSpec((1, TOKEN_TILE, d), lambda i, j: (i, j, 0)),
            _resident((1, d)),
            _resident((1, d)),
            _resident((d, D_FF)),
            _resident((D_FF, d)),
            _resident((1, d)),
        ],
        out_specs=pl.BlockSpec((1, TOKEN_TILE, d), lambda i, j: (i, j, 0)),
        out_shape=jax.ShapeDtypeStruct(x.shape, F32),
        compiler_params=_params(2),
        name=name,
    )(mix, u, kv, w_out, x, g.reshape(1, d), g1.reshape(1, d), w1, w2, g2.reshape(1, d))


def _lambda_init(layer_idx):
    return 0.8 - 0.6 * math.exp(-0.3 * layer_idx)


def _group_dt_layout(a, b, s):
    a = a.reshape(b, 2, 2, SSD_N_GROUPS, SSD_HEADS_PER_GROUP, s)
    return jnp.transpose(a, (0, 3, 1, 2, 4, 5)).reshape(b, SSD_N_GROUPS, 16, s)


def _trunk(x, mem, p):
    b, s, d = x.shape
    tokens = b * s

    def kv_proj(i):
        kv = _norm_proj(mem.reshape(b * MEM_LEN, d), p["x_mem_norm"][i], p["x_w_kv"][i],
                        [(2 * X_WIDTH, BF16)], name=f"kv_proj{i}")[0]
        return kv.reshape(b, MEM_LEN, 2 * X_WIDTH)

    def layer_tail(i, mix, u, w_out, x, name):
        return _layer_tail(mix, u, u.shape[-1] // X_WIDTH - 1, kv_proj(i), w_out, x,
                           p["norm_post_mix"][i], p["norm_pre_mlp"][i], p["mlp_w1"][i], p["mlp_w2"][i],
                           p["norm_post_mlp"][i], name=name)

    u_main, dt_raw = _norm_proj(
        x.reshape(tokens, d), p["norm_pre_mix"][0], p["ssd_w_in"],
        [(SSD_MAIN_COLS, BF16), (SSD_DT_PAD, F32)], name="ssd_in_proj")
    u_main = u_main.reshape(b, s, SSD_MAIN_COLS)
    dt_t = jnp.transpose(dt_raw.reshape(b, s, SSD_DT_PAD)[:, :, :SSD_DT_COLS], (0, 2, 1))
    col_arr, row_arr = _dt_prep(dt_t, p["ssd_dt_bias"], p["ssd_a_log"])
    a_col = jnp.transpose(_group_dt_layout(col_arr, b, s), (0, 1, 3, 2))
    a_row = _group_dt_layout(row_arr, b, s)
    mix = _ssd(u_main, a_col, a_row, p["ssd_conv_w"], p["ssd_conv_b"], p["ssd_d_e"], p["ssd_norm"])
    x = layer_tail(0, mix, u_main, p["ssd_w_out"], x, "ssd_layer_tail")

    u, v_t = _norm_proj(x.reshape(tokens, d), p["norm_pre_mix"][1], p["diff_w_in"],
                        [(p["diff_w_in"].shape[1], BF16)], name="diff_in_proj", w_t=p["diff_w_v_t"])
    u = u.reshape(b, s, -1)
    mix = _diff_attn(u, v_t, p["rel_bias_table"], p["bias_tiles"], p["diff_lambda"], p["diff_subln"],
                     _lambda_init(1))
    return layer_tail(1, mix, u, p["diff_w_out"], x, "diff_layer_tail")


def kernel(x_prompt, x_sample, mem_prompt, mem_sample, rel_bias_table, norm_pre_mix, norm_post_mix,
           norm_pre_mlp, norm_post_mlp, ssd_w_in, ssd_conv_w, ssd_conv_b, ssd_dt_bias, ssd_a_log, ssd_d,
           ssd_norm, ssd_w_out, diff_w_in, diff_lambda, diff_subln, diff_w_out, x_mem_norm, x_w_kv,
           mlp_w1, mlp_w2):
    w_in = ssd_w_in[0]
    mix_cols = SSD_D_INNER + SSD_XBC
    w_in = jnp.concatenate([
        w_in[:, :mix_cols], w_in[:, mix_cols + SSD_DT_COLS:], w_in[:, mix_cols:mix_cols + SSD_DT_COLS],
        jnp.zeros((D_MODEL, SSD_DT_PAD - SSD_DT_COLS), w_in.dtype)], axis=1)
    p = {
        "rel_bias_table": rel_bias_table,
        "bias_tiles": _bias_tiles(rel_bias_table),
        "norm_pre_mix": norm_pre_mix, "norm_post_mix": norm_post_mix,
        "norm_pre_mlp": norm_pre_mlp, "norm_post_mlp": norm_post_mlp,
        "ssd_w_in": w_in.astype(BF16),
        "ssd_conv_w": ssd_conv_w[0], "ssd_conv_b": ssd_conv_b[0].reshape(1, SSD_XBC),
        "ssd_dt_bias": ssd_dt_bias[0], "ssd_a_log": ssd_a_log[0],
        "ssd_d_e": jnp.repeat(ssd_d[0], SSD_HEAD_DIM).reshape(1, SSD_D_INNER),
        "ssd_norm": ssd_norm[0].reshape(1, SSD_D_INNER),
        "ssd_w_out": ssd_w_out[0].astype(BF16),
        "diff_w_in": jnp.concatenate([
            diff_w_in[0][:, :DIFF_QK_COLS] * (DIFF_HEAD_DIM ** -0.5 * LOG2E),
            diff_w_in[0][:, DIFF_QK_COLS:2 * DIFF_QK_COLS],
            diff_w_in[0][:, 2 * DIFF_QK_COLS + DIFF_WIDTH:]], axis=1).astype(BF16),
        "diff_w_v_t": diff_w_in[0][:, 2 * DIFF_QK_COLS:2 * DIFF_QK_COLS + DIFF_WIDTH].T.astype(BF16),
        "diff_lambda": diff_lambda[0], "diff_subln": diff_subln[0],
        "diff_w_out": diff_w_out[0].astype(BF16),
        "x_mem_norm": x_mem_norm, "x_w_kv": x_w_kv.astype(BF16),
        "mlp_w1": mlp_w1.astype(BF16), "mlp_w2": mlp_w2.astype(BF16),
    }
    return (_trunk(x_prompt, mem_prompt, p), _trunk(x_sample, mem_sample, p))
```

```python
import functools
import math

import jax
import jax.numpy as jnp
from jax import lax
from jax.experimental import pallas as pl
from jax.experimental.pallas import tpu as pltpu

F32 = jnp.float32
BF16 = jnp.bfloat16

EPS = 1e-6
D_MODEL = 1024
MEM_LEN = 256

SSD_D_INNER = 2048
SSD_HEAD_DIM = 64
SSD_N_HEADS = 32
SSD_N_GROUPS = 8
SSD_HEADS_PER_GROUP = SSD_N_HEADS // SSD_N_GROUPS
SSD_D_STATE = 128
SSD_BC = SSD_N_GROUPS * SSD_D_STATE
SSD_XBC = SSD_D_INNER + 2 * SSD_BC
SSD_CONV_WIDTH = 5
SSD_CHUNK = 128
SSD_GROUP_WIDTH = SSD_D_INNER // SSD_N_GROUPS
SSD_MAIN_COLS = SSD_D_INNER + SSD_XBC + 1024
SSD_DT_COLS = 2 * SSD_N_HEADS
SSD_DT_PAD = 128
SSD_OUT_UNROLL = 8
SSD_LOCAL_ITERS = 2

DIFF_N_HEADS = 8
DIFF_HEAD_DIM = 64
DIFF_V_DIM = 128
DIFF_QK_COLS = 1024
DIFF_WIDTH = 1024
REL_BUCKETS = 32

X_N_HEADS = 4
X_HEAD_DIM = 256
X_WIDTH = 1024
D_FF = 4096

VMEM_LIMIT_BYTES = 56 * 1024 * 1024

TOKEN_TILE = 512
ATTN_TQ = 512
ATTN_TK = 256
ATTN_DOT_KEYS = 512
ATTN_Q_UNROLL = 2
ATTN_Q_FULL_UNROLL_MAX = 4
ATTN_OFF_LO = -2
ATTN_OFF_HI = ATTN_TQ // ATTN_TK + 1
ATTN_BIAS_TILES = ATTN_OFF_HI - ATTN_OFF_LO + 1
ATTN_BOUND_SLACK = 1.02
ATTN_MIN_DENOM = 2.0 ** -60
NEG_BIG = -1e30
LOG2E = math.log2(math.e)


def _params(n_grid_dims):
    return pltpu.CompilerParams(
        dimension_semantics=("arbitrary",) * n_grid_dims,
        vmem_limit_bytes=VMEM_LIMIT_BYTES,
    )


def _rms(x):
    return x * lax.rsqrt(jnp.mean(x * x, axis=-1, keepdims=True) + EPS)


def _silu(x):
    h = 0.5 * x
    return h + h * jnp.tanh(h)


def _resident(shape):
    nd = len(shape)
    return pl.BlockSpec(shape, lambda *_: (0,) * nd, pipeline_mode=pl.Buffered(1))


def _norm_proj_kernel(x_ref, g_ref, w_ref, *rest, col_chunk, has_t):
    h = (_rms(x_ref[...]) * g_ref[...]).astype(BF16)
    o_refs = rest[1:] if has_t else rest
    if has_t:
        wt_ref, ot_ref = rest[0], rest[-1]
        o_refs = o_refs[:-1]
        ot_ref[...] = lax.dot_general(wt_ref[...], h, (((1,), (1,)), ((), ())),
                                      preferred_element_type=F32).astype(ot_ref.dtype)
    c0 = 0
    for o_ref in o_refs:
        n = o_ref.shape[-1]
        for j in range(0, n, col_chunk):
            cw = min(col_chunk, n - j)
            o_ref[:, j:j + cw] = jnp.dot(
                h, w_ref[:, c0 + j:c0 + j + cw], preferred_element_type=F32
            ).astype(o_ref.dtype)
        c0 += n


def _norm_proj(x2d, g, w, outs, name, w_t=None):
    m, d = x2d.shape
    n = w.shape[1]
    assert sum(wd for wd, _ in outs) == n and m % TOKEN_TILE == 0
    in_specs = [
        pl.BlockSpec((TOKEN_TILE, d), lambda i: (i, 0)),
        _resident((1, d)),
        _resident((d, n)),
    ]
    out_specs = [pl.BlockSpec((TOKEN_TILE, wd), lambda i: (i, 0)) for wd, _ in outs]
    out_shape = [jax.ShapeDtypeStruct((m, wd), dt) for wd, dt in outs]
    args = [x2d, g.reshape(1, d), w]
    if w_t is not None:
        in_specs.append(_resident(w_t.shape))
        out_specs.append(pl.BlockSpec((w_t.shape[0], TOKEN_TILE), lambda i: (0, i)))
        out_shape.append(jax.ShapeDtypeStruct((w_t.shape[0], m), BF16))
        args.append(w_t)
    return pl.pallas_call(
        functools.partial(_norm_proj_kernel, col_chunk=1024, has_t=w_t is not None),
        grid=(m // TOKEN_TILE,),
        in_specs=in_specs,
        out_specs=out_specs,
        out_shape=out_shape,
        compiler_params=_params(1),
        name=name,
    )(*args)


def _dt_prep_kernel(dt_ref, bias_ref, alog_ref, col_ref, row_ref, *, seq):
    n = SSD_DT_COLS
    nh = SSD_N_HEADS
    row = lax.broadcasted_iota(jnp.int32, (SSD_CHUNK, SSD_CHUNK), 0)
    col = lax.broadcasted_iota(jnp.int32, (SSD_CHUNK, SSD_CHUNK), 1)
    prefix = (row <= col).astype(F32)
    suffix = (row >= col).astype(F32)
    is_fwd = lax.broadcasted_iota(jnp.int32, (n, SSD_CHUNK), 0) < nh
    a = -jnp.exp(alog_ref[...])
    for c in range(seq // SSD_CHUNK):
        sl = slice(c * SSD_CHUNK, (c + 1) * SSD_CHUNK)
        dt = jax.nn.softplus(dt_ref[0, :, sl] + bias_ref[...])
        da = dt * a
        cum_f = jnp.dot(da, prefix, preferred_element_type=F32, precision=lax.Precision.HIGHEST)
        cum_b = jnp.dot(da, suffix, preferred_element_type=F32, precision=lax.Precision.HIGHEST)
        cum = jnp.where(is_fwd, cum_f, cum_b)
        col_ref[0, 0:n, sl] = cum
        col_ref[0, n:2 * n, sl] = dt
        row_ref[0, 0:n, sl] = cum - jnp.log(dt)
        row_ref[0, n:n + nh, sl] = jnp.log(dt[0:nh] + dt[nh:n])
        row_ref[0, n + nh:2 * n, sl] = jnp.zeros((nh, SSD_CHUNK), F32)


def _dt_prep(dt_t, dt_bias, a_log):
    b, n, s = dt_t.shape
    return pl.pallas_call(
        functools.partial(_dt_prep_kernel, seq=s),
        grid=(b,),
        in_specs=[
            pl.BlockSpec((1, n, s), lambda i: (i, 0, 0)),
            _resident((n, 1)),
            _resident((n, 1)),
        ],
        out_specs=[pl.BlockSpec((1, 2 * n, s), lambda i: (i, 0, 0))] * 2,
        out_shape=[jax.ShapeDtypeStruct((b, 2 * n, s), F32)] * 2,
        compiler_params=_params(1),
        name="ssd_dt_prep",
    )(dt_t, dt_bias.reshape(n, 1), a_log.reshape(n, 1))


def _ssd_kernel(z_ref, x_ref, b_ref, c_ref, acol_ref, arow_ref,
                cwx_ref, cwb_ref, cwc_ref, cbx_ref, cbb_ref, cbc_ref, dskip_ref, ng_ref,
                o_ref, dec_scr, cs_scr, y_scr, hf_scr, hb_scr, *, seq):
    L = SSD_CHUNK
    R = SSD_HEADS_PER_GROUP
    GW = SSD_GROUP_WIDTH
    NS = SSD_D_STATE
    KW = SSD_CONV_WIDTH
    nc = seq // L
    halo = 16
    nt = (((1,), (1,)), ((), ()))

    lane_head = lax.broadcasted_iota(jnp.int32, (1, GW), 1) // SSD_HEAD_DIM
    row_i = lax.broadcasted_iota(jnp.int32, (L, L), 0)
    col_i = lax.broadcasted_iota(jnp.int32, (L, L), 1)
    e_row = lax.broadcasted_iota(jnp.int32, (3 * 2 * R, 2 * GW), 0)
    e_col = lax.broadcasted_iota(jnp.int32, (3 * 2 * R, 2 * GW), 1)
    expand_mat = jnp.where(e_row % (2 * R) == e_col // SSD_HEAD_DIM, 1.0, 0.0).astype(BF16)

    def expand(cols):
        hi = cols.astype(BF16)
        r1 = cols - hi.astype(F32)
        mid = r1.astype(BF16)
        lo = (r1 - mid.astype(F32)).astype(BF16)
        return jnp.dot(jnp.concatenate([hi, mid, lo], axis=1), expand_mat, preferred_element_type=F32)

    mid_tap = KW // 2
    seg_rows = [L + halo] * mid_tap + [L] + [L + halo] * mid_tap
    seg_off = [halo] * mid_tap + [0] * (mid_tap + 1)
    s_row = lax.broadcasted_iota(jnp.int32, (L, sum(seg_rows)), 0)
    s_col = lax.broadcasted_iota(jnp.int32, (L, sum(seg_rows)), 1)
    hit = jnp.zeros((L, sum(seg_rows)), jnp.bool_)
    k0 = 0
    for w in range(KW):
        hit = hit | (s_col == k0 + seg_off[w] + s_row + w - mid_tap)
        k0 += seg_rows[w]
    shift_mat = jnp.where(hit, 1.0, 0.0).astype(BF16)

    def conv_stage(c):
        c0 = pl.multiple_of(c * L, L)
        p0 = pl.multiple_of(jnp.maximum(c0 - halo, 0), halo)
        n0 = pl.multiple_of(jnp.minimum(c0 + L, seq - halo), halo)

        def rows(r0, n):
            return jnp.concatenate([ref[0, pl.ds(r0, n), :] for ref in (x_ref, b_ref, c_ref)], axis=1)

        main = rows(c0, L)
        prev = rows(p0, halo)
        nxt = rows(n0, halo)
        prev = jnp.where(c > 0, prev, jnp.zeros_like(prev))
        nxt = jnp.where(c < nc - 1, nxt, jnp.zeros_like(nxt))
        before = jnp.concatenate([prev, main], axis=0)
        after = jnp.concatenate([main, nxt], axis=0)
        cw = jnp.concatenate([cwx_ref[...], cwb_ref[...], cwc_ref[...]], axis=1).astype(BF16)
        cb = jnp.concatenate([cbx_ref[...], cbb_ref[...], cbc_ref[...]], axis=1)
        segs = [before] * mid_tap + [main] + [after] * mid_tap
        scaled = jnp.concatenate([segs[w] * cw[w:w + 1, :] for w in range(KW)], axis=0)
        act = _silu(jnp.dot(shift_mat, scaled, preferred_element_type=F32) + cb)
        bcf = act[:, GW:GW + NS]
        return act[:, :GW], bcf.astype(BF16), act[:, GW + NS:].astype(BF16), bcf.T.astype(BF16)

    def decay_stage(c, vals):
        xc, bc, cc, bct = vals
        c0 = pl.multiple_of(c * L, L)
        g = lax.dot_general(cc, bc, nt, preferred_element_type=F32)
        acol = acol_ref[0, 0, pl.ds(c0, L), :]
        arow = arow_ref[0, 0, :, pl.ds(c0, L)]
        ws = []
        for r in range(R):
            arg = jnp.where(row_i > col_i,
                            acol[:, r:r + 1] - arow[r:r + 1, :],
                            acol[:, R + r:R + r + 1] - arow[R + r:R + r + 1, :])
            arg = jnp.where(row_i == col_i, arow[2 * R + r:2 * R + r + 1, :], arg)
            ws.append((g * jnp.exp(arg)).astype(BF16))
        return xc, cc, bct, jnp.concatenate(ws, axis=1)

    def local_stage(c, vals):
        xc, cc, bct, wcat = vals
        c0 = pl.multiple_of(c * L, L)
        cs_scr[pl.ds(c0, L), :] = cc
        acol = acol_ref[0, 0, pl.ds(c0, L), :]
        cum = acol[:, 0:2 * R]
        lane8 = lax.broadcasted_iota(jnp.int32, (1, 2 * R), 1)
        tot = jnp.where(lane8 < R, cum[L - 1:L, :], cum[0:1, :])
        ss = expand(jnp.concatenate([jnp.exp(tot - cum) * acol[:, 2 * R:4 * R],
                                     jnp.broadcast_to(jnp.exp(tot), (8, 2 * R))], axis=0))
        xb = xc.astype(BF16)
        xbd = jnp.concatenate([jnp.where(lane_head == r, xb, jnp.zeros_like(xb)) for r in range(R)], axis=0)
        y = jnp.dot(wcat, xbd, preferred_element_type=F32)
        y_scr[pl.ds(c0, L), :] = y + xc * dskip_ref[...]
        dec_scr[c] = ss[L:, :]
        xsc = (jnp.concatenate([xc, xc], axis=1) * ss[:L, :]).astype(BF16)
        s_loc = jnp.dot(bct, xsc, preferred_element_type=F32)
        hf_scr[c] = s_loc[:, :GW]
        hb_scr[c] = s_loc[:, GW:]

    def conv_pair(p):
        return conv_stage(2 * p), conv_stage(2 * p + 1)

    def decay_pair(p, vals):
        return decay_stage(2 * p, vals[0]), decay_stage(2 * p + 1, vals[1])

    def local_pair(p, vals):
        local_stage(2 * p, vals[0])
        local_stage(2 * p + 1, vals[1])

    def local_body(p, carry):
        dec_vals, conv_vals = carry
        nxt_conv = conv_pair(p + 2)
        nxt_dec = decay_pair(p + 1, conv_vals)
        local_pair(p, dec_vals)
        return nxt_dec, nxt_conv

    npair = nc // 2
    carry = (decay_pair(0, conv_pair(0)), conv_pair(1))
    steps = (npair - 2) // SSD_LOCAL_ITERS

    def local_multi(i, carry):
        for k in range(steps):
            carry = local_body(steps * i + k, carry)
        return carry

    dec_vals, conv_vals = lax.fori_loop(0, SSD_LOCAL_ITERS, local_multi, carry)
    last = decay_pair(npair - 1, conv_vals)
    local_pair(npair - 2, dec_vals)
    local_pair(npair - 1, last)

    def state_body(i, carry):
        sf, sb = carry
        cb = nc - 1 - i
        s_loc = hf_scr[i]
        hf_scr[i] = sf
        sf = sf * dec_scr[i][0:1, :GW] + s_loc
        s_loc = hb_scr[cb]
        hb_scr[cb] = sb
        sb = sb * dec_scr[cb][0:1, GW:] + s_loc
        return sf, sb

    zero = jnp.zeros((NS, GW), F32)
    lax.fori_loop(0, nc, state_body, (zero, zero))

    def out_chunk(c):
        c0 = pl.multiple_of(c * L, L)
        cc = cs_scr[pl.ds(c0, L), :]
        h = jnp.concatenate([hf_scr[c], hb_scr[c]], axis=1).astype(BF16)
        yi = jnp.dot(cc, h, preferred_element_type=F32)
        yi = yi * expand(jnp.exp(acol_ref[0, 0, pl.ds(c0, L), 0:2 * R]))
        y = y_scr[pl.ds(c0, L), :] + yi[:, :GW] + yi[:, GW:]
        y = y * _silu(z_ref[0, pl.ds(c0, L), :].astype(F32))
        o_ref[0, pl.ds(c0, L), :] = (_rms(y) * ng_ref[...]).astype(o_ref.dtype)

    def out_body(i, carry):
        for j in range(SSD_OUT_UNROLL):
            out_chunk(SSD_OUT_UNROLL * i + j)
        return carry

    lax.fori_loop(0, nc // SSD_OUT_UNROLL, out_body, 0)


def _ssd(u_main, a_col, a_row, conv_w, conv_b, d_skip_e, norm_g):
    b, s, _ = u_main.shape
    gw, ds = SSD_GROUP_WIDTH, SSD_D_STATE
    nc = s // SSD_CHUNK
    assert nc % SSD_OUT_UNROLL == 0 and nc % 2 == 0 and (nc // 2 - 2) % SSD_LOCAL_ITERS == 0 and nc >= 8
    x_blk0 = SSD_D_INNER // gw
    b_blk0 = (2 * SSD_D_INNER) // ds
    c_blk0 = (2 * SSD_D_INNER + SSD_BC) // ds
    cw_b0 = SSD_D_INNER // ds
    cw_c0 = (SSD_D_INNER + SSD_BC) // ds
    kw = SSD_CONV_WIDTH
    return pl.pallas_call(
        functools.partial(_ssd_kernel, seq=s),
        grid=(b, SSD_N_GROUPS),
        in_specs=[
            pl.BlockSpec((1, s, gw), lambda i, g: (i, 0, g)),
            pl.BlockSpec((1, s, gw), lambda i, g: (i, 0, x_blk0 + g)),
            pl.BlockSpec((1, s, ds), lambda i, g: (i, 0, b_blk0 + g)),
            pl.BlockSpec((1, s, ds), lambda i, g: (i, 0, c_blk0 + g)),
            pl.BlockSpec((1, 1, s, 16), lambda i, g: (i, g, 0, 0)),
            pl.BlockSpec((1, 1, 16, s), lambda i, g: (i, g, 0, 0)),
            pl.BlockSpec((kw, gw), lambda i, g: (0, g)),
            pl.BlockSpec((kw, ds), lambda i, g: (0, cw_b0 + g)),
            pl.BlockSpec((kw, ds), lambda i, g: (0, cw_c0 + g)),
            pl.BlockSpec((1, gw), lambda i, g: (0, g)),
            pl.BlockSpec((1, ds), lambda i, g: (0, cw_b0 + g)),
            pl.BlockSpec((1, ds), lambda i, g: (0, cw_c0 + g)),
            pl.BlockSpec((1, gw), lambda i, g: (0, g)),
            pl.BlockSpec((1, gw), lambda i, g: (0, g)),
        ],
        out_specs=pl.BlockSpec((1, s, gw), lambda i, g: (i, 0, g)),
        out_shape=jax.ShapeDtypeStruct((b, s, SSD_D_INNER), BF16),
        scratch_shapes=[
            pltpu.VMEM((nc, 8, 2 * gw), F32),
            pltpu.VMEM((s, ds), BF16),
            pltpu.VMEM((s, gw), F32),
            pltpu.VMEM((nc, ds, gw), F32),
            pltpu.VMEM((nc, ds, gw), F32),
        ],
        compiler_params=_params(2),
        name="ssd_scan",
    )(u_main, u_main, u_main, u_main, a_col, a_row,
      conv_w, conv_w, conv_w, conv_b, conv_b, conv_b, d_skip_e, norm_g)


_BUCKET_STEPS = (12, 16, 23, 32, 46, 64, 91)


def _bias_tile_kernel(table_ref, o_ref):
    d = pl.program_id(0) + ATTN_OFF_LO
    k = lax.broadcasted_iota(jnp.int32, (ATTN_TK, ATTN_TQ), 0)
    q = lax.broadcasted_iota(jnp.int32, (ATTN_TK, ATTN_TQ), 1)
    rel = k + d * ATTN_TK - q
    n = jnp.abs(rel)
    large = jnp.full_like(n, 8)
    for step in _BUCKET_STEPS:
        large = large + (n >= step).astype(jnp.int32)
    bucket = jnp.where(rel > 0, 16, 0) + jnp.where(n < 8, n, large)
    for h in range(DIFF_N_HEADS):
        acc = jnp.zeros((ATTN_TK, ATTN_TQ), F32)
        for bkt in range(REL_BUCKETS):
            acc = jnp.where(bucket == bkt, table_ref[bkt, h] * LOG2E, acc)
        o_ref[0, h] = acc


def _bias_tiles(table):
    return pl.pallas_call(
        _bias_tile_kernel,
        grid=(ATTN_BIAS_TILES,),
        in_specs=[pl.BlockSpec(memory_space=pltpu.SMEM)],
        out_specs=pl.BlockSpec((1, DIFF_N_HEADS, ATTN_TK, ATTN_TQ), lambda i: (i, 0, 0, 0)),
        out_shape=jax.ShapeDtypeStruct((ATTN_BIAS_TILES, DIFF_N_HEADS, ATTN_TK, ATTN_TQ), F32),
        compiler_params=_params(1),
        name="rel_bias_tiles",
    )(table)


def _diff_attn_kernel(table_ref, lam_ref, q_ref, k_ref, vt_ref, bias_ref, g_ref, o_ref,
                      *, seq, lam_init):
    tq, tk, dk = ATTN_TQ, ATTN_TK, ATTN_DOT_KEYS
    nq = seq // tq
    h = pl.program_id(1)
    nt = (((1,), (1,)), ((), ()))
    hd = DIFF_HEAD_DIM
    lane = lax.broadcasted_iota(jnp.int32, (1, 2 * hd), 1)
    sel_row = lax.broadcasted_iota(jnp.int32, (8, 2 * hd), 0)
    sel_lane = lax.broadcasted_iota(jnp.int32, (8, 2 * hd), 1)
    sel = jnp.where(sel_row == sel_lane // hd, 1.0, 0.0).astype(BF16)

    def half_sq_norms(x):
        xf = x.astype(F32)
        return lax.dot_general(sel, (xf * xf).astype(BF16), nt, preferred_element_type=F32)

    kmax = half_sq_norms(k_ref[0, 0:dk, :])
    for c in range(1, seq // dk):
        kmax = jnp.maximum(kmax, half_sq_norms(k_ref[0, c * dk:(c + 1) * dk, :]))
    kmax = jnp.max(kmax, axis=1, keepdims=True)

    bmax = table_ref[0, h]
    for bkt in range(1, REL_BUCKETS):
        bmax = jnp.maximum(bmax, table_ref[bkt, h])
    bmax = bmax * LOG2E
    far_lo = table_ref[REL_BUCKETS // 2 - 1, h] * LOG2E
    far_hi = table_ref[REL_BUCKETS - 1, h] * LOG2E
    n_grp = seq // dk

    lp = lam_ref[...]
    lam = (jnp.exp(jnp.sum(lp[0:1] * lp[1:2], axis=-1, keepdims=True))
           - jnp.exp(jnp.sum(lp[2:3] * lp[3:4], axis=-1, keepdims=True)) + lam_init)

    def q_block(qi):
        q0 = pl.multiple_of(qi * tq, tq)
        q = q_ref[0, pl.ds(q0, tq), :]
        zero = jnp.zeros_like(q)
        q1 = jnp.where(lane < hd, q, zero)
        q2 = jnp.where(lane >= hd, q, zero)

        near = (0, 1, -1)
        order = list(near) + list(range(2, n_grp - 1))

        def group(dj):
            return lax.rem(qi + dj + n_grp, n_grp)

        def qk(j):
            kt = k_ref[0, pl.ds(pl.multiple_of(j * dk, dk), dk), :]
            return (lax.dot_general(kt, q1, nt, preferred_element_type=F32),
                    lax.dot_general(kt, q2, nt, preferred_element_type=F32))

        def bias_tiles(j):
            kb0 = j * (dk // tk) - (tq // tk) * qi
            return jnp.concatenate(
                [bias_ref[jnp.clip(kb0 + i, ATTN_OFF_LO, ATTN_OFF_HI) - ATTN_OFF_LO, 0]
                 for i in range(dk // tk)], axis=0)

        def far_bias(j):
            return jnp.where(j > qi, far_hi, far_lo)

        def accumulate(sh1, sh2):
            l1 = jnp.zeros((1, tq), F32)
            l2 = jnp.zeros((1, tq), F32)
            a1 = jnp.zeros((DIFF_V_DIM, tq), F32)
            a2 = jnp.zeros((DIFF_V_DIM, tq), F32)
            nxt = qk(group(order[0]))
            for idx, dj in enumerate(order):
                j = group(dj)
                s1, s2 = nxt
                if idx + 1 < n_grp:
                    nxt = qk(group(order[idx + 1]))
                if dj in near:
                    bias = bias_tiles(j)
                    p1 = jnp.exp2(s1 + bias - sh1)
                    p2 = jnp.exp2(s2 + bias - sh2)
                else:
                    c = far_bias(j)
                    p1 = jnp.exp2(s1 - (sh1 - c))
                    p2 = jnp.exp2(s2 - (sh2 - c))
                vt = vt_ref[:, pl.ds(pl.multiple_of(j * dk, dk), dk)]
                l1 = l1 + jnp.sum(p1, axis=0, keepdims=True)
                l2 = l2 + jnp.sum(p2, axis=0, keepdims=True)
                a1 = a1 + jnp.dot(vt, p1.astype(BF16), preferred_element_type=F32)
                a2 = a2 + jnp.dot(vt, p2.astype(BF16), preferred_element_type=F32)
            return l1, a1, l2, a2

        def exact_max():
            m1 = jnp.full((1, tq), NEG_BIG, F32)
            m2 = jnp.full((1, tq), NEG_BIG, F32)
            for dj in order:
                j = group(dj)
                s1, s2 = qk(j)
                if dj in near:
                    bias = bias_tiles(j)
                    b1 = jnp.max(s1 + bias, axis=0, keepdims=True)
                    b2 = jnp.max(s2 + bias, axis=0, keepdims=True)
                else:
                    c = far_bias(j)
                    b1 = jnp.max(s1, axis=0, keepdims=True) + c
                    b2 = jnp.max(s2, axis=0, keepdims=True) + c
                m1 = jnp.maximum(m1, b1)
                m2 = jnp.maximum(m2, b2)
            return m1, m2

        def finish(l1, a1, l2, a2):
            out = a1 * (1.0 / l1) - lam * (a2 * (1.0 / l2))
            out = out * lax.rsqrt(jnp.mean(out * out, axis=0, keepdims=True) + EPS)
            out = out * (g_ref[...] * (1.0 - lam_init))
            o_ref[0, pl.ds(q0, tq), :] = out.T.astype(o_ref.dtype)

        bound = jnp.sqrt(half_sq_norms(q) * kmax) * ATTN_BOUND_SLACK + bmax
        l1, a1, l2, a2 = accumulate(bound[0:1], bound[1:2])
        finish(l1, a1, l2, a2)
        ok = jnp.minimum(jnp.min(l1), jnp.min(l2)) > ATTN_MIN_DENOM
        return ok, lambda: finish(*accumulate(*exact_max()))

    unroll = nq if nq <= ATTN_Q_FULL_UNROLL_MAX else ATTN_Q_UNROLL

    def q_body(i, carry):
        blocks = [q_block(unroll * i + u) for u in range(unroll)]
        for ok, redo in blocks:
            pl.when(jnp.logical_not(ok))(redo)
        return carry

    lax.fori_loop(0, nq // unroll, q_body, 0)


def _diff_attn(u, v_t, table, bias_tiles, lam_params, subln_g, lam_init):
    b, s, _ = u.shape
    assert ATTN_DOT_KEYS == ATTN_TQ and ATTN_DOT_KEYS % ATTN_TK == 0
    assert s % (ATTN_TQ * ATTN_Q_UNROLL) == 0 and s // ATTN_DOT_KEYS >= 4
    hd2 = 2 * DIFF_HEAD_DIM
    k_blk0 = DIFF_QK_COLS // hd2
    return pl.pallas_call(
        functools.partial(_diff_attn_kernel, seq=s, lam_init=lam_init),
        grid=(b, DIFF_N_HEADS),
        in_specs=[
            pl.BlockSpec(memory_space=pltpu.SMEM),
            _resident((4, DIFF_HEAD_DIM)),
            pl.BlockSpec((1, s, hd2), lambda i, h: (i, 0, h)),
            pl.BlockSpec((1, s, hd2), lambda i, h: (i, 0, k_blk0 + h)),
            pl.BlockSpec((DIFF_V_DIM, s), lambda i, h: (h, i)),
            pl.BlockSpec((ATTN_BIAS_TILES, 1, ATTN_TK, ATTN_TQ), lambda i, h: (0, h, 0, 0)),
            _resident((DIFF_V_DIM, 1)),
        ],
        out_specs=pl.BlockSpec((1, s, DIFF_V_DIM), lambda i, h: (i, 0, h)),
        out_shape=jax.ShapeDtypeStruct((b, s, DIFF_WIDTH), BF16),
        compiler_params=_params(2),
        name="diff_attn",
    )(table, lam_params, u, u, v_t, bias_tiles, subln_g.reshape(DIFF_V_DIM, 1))


def _layer_tail_kernel(mix_ref, q_ref, kv_ref, w_ref, x_ref, g_ref, g1_ref, w1_ref, w2_ref, g2_ref,
                       o_ref, *, k_mix, ff_chunk):
    q = q_ref[0]
    kv = kv_ref[0]
    nt = (((1,), (1,)), ((), ()))
    scale = X_HEAD_DIM ** -0.5
    heads = [slice(h * X_HEAD_DIM, (h + 1) * X_HEAD_DIM) for h in range(X_N_HEADS)]
    logits = [lax.dot_general(q[:, cs], kv[:, cs], nt, preferred_element_type=F32) * scale for cs in heads]
    o = jnp.dot(mix_ref[0], w_ref[0:k_mix, :], preferred_element_type=F32)
    mems = []
    for h, cs in enumerate(heads):
        vs = slice(X_WIDTH + cs.start, X_WIDTH + cs.stop)
        p = jnp.exp(logits[h] - jnp.max(logits[h], axis=-1, keepdims=True))
        l = jnp.sum(p, axis=-1, keepdims=True)
        mem = jnp.dot(p.astype(BF16), kv[:, vs], preferred_element_type=F32) * (1.0 / l)
        mems.append(mem.astype(BF16))
    o = o + jnp.dot(jnp.concatenate(mems, axis=1), w_ref[k_mix:k_mix + X_WIDTH, :],
                    preferred_element_type=F32)
    x = x_ref[0] + _rms(o) * g_ref[...]

    h = (_rms(x) * g1_ref[...]).astype(BF16)
    f = None
    for c in range(0, D_FF, ff_chunk):
        a = jnp.dot(h, w1_ref[:, c:c + ff_chunk], preferred_element_type=F32)
        a = jnp.square(jnp.maximum(a, 0.0)).astype(BF16)
        part = jnp.dot(a, w2_ref[c:c + ff_chunk, :], preferred_element_type=F32)
        f = part if f is None else f + part
    o_ref[0] = x + _rms(f) * g2_ref[...]


def _layer_tail(mix, u, q_blk, kv, w_out, x, g, g1, w1, w2, g2, name):
    b, s, k_mix = mix.shape
    d = x.shape[-1]
    return pl.pallas_call(
        functools.partial(_layer_tail_kernel, k_mix=k_mix, ff_chunk=1024),
        grid=(b, s // TOKEN_TILE),
        in_specs=[
            pl.BlockSpec((1, TOKEN_TILE, k_mix), lambda i, j: (i, j, 0)),
            pl.BlockSpec((1, TOKEN_TILE, X_WIDTH), lambda i, j: (i, j, q_blk)),
            pl.BlockSpec((1, MEM_LEN, 2 * X_WIDTH), lambda i, j: (i, 0, 0)),
            _resident((k_mix + X_WIDTH, d)),
            pl.BlockSpec((1, TOKEN_TILE, d), lambda i, j: (i, j, 0)),
            _resident((1, d)),
            _resident((1, d)),
            _resident((d, D_FF)),
            _resident((D_FF, d)),
            _resident((1, d)),
        ],
        out_specs=pl.BlockSpec((1, TOKEN_TILE, d), lambda i, j: (i, j, 0)),
        out_shape=jax.ShapeDtypeStruct(x.shape, F32),
        compiler_params=_params(2),
        name=name,
    )(mix, u, kv, w_out, x, g.reshape(1, d), g1.reshape(1, d), w1, w2, g2.reshape(1, d))


def _lambda_init(layer_idx):
    return 0.8 - 0.6 * math.exp(-0.3 * layer_idx)


def _group_dt_layout(a, b, s):
    a = a.reshape(b, 2, 2, SSD_N_GROUPS, SSD_HEADS_PER_GROUP, s)
    return jnp.transpose(a, (0, 3, 1, 2, 4, 5)).reshape(b, SSD_N_GROUPS, 16, s)


def _trunk(x, mem, p):
    b, s, d = x.shape
    tokens = b * s

    def kv_proj(i):
        kv = _norm_proj(mem.reshape(b * MEM_LEN, d), p["x_mem_norm"][i], p["x_w_kv"][i],
                        [(2 * X_WIDTH, BF16)], name=f"kv_proj{i}")[0]
        return kv.reshape(b, MEM_LEN, 2 * X_WIDTH)

    def layer_tail(i, mix, u, w_out, x, name):
        return _layer_tail(mix, u, u.shape[-1] // X_WIDTH - 1, kv_proj(i), w_out, x,
                           p["norm_post_mix"][i], p["norm_pre_mlp"][i], p["mlp_w1"][i], p["mlp_w2"][i],
                           p["norm_post_mlp"][i], name=name)

    u_main, dt_raw = _norm_proj(
        x.reshape(tokens, d), p["norm_pre_mix"][0], p["ssd_w_in"],
        [(SSD_MAIN_COLS, BF16), (SSD_DT_PAD, F32)], name="ssd_in_proj")
    u_main = u_main.reshape(b, s, SSD_MAIN_COLS)
    dt_t = jnp.transpose(dt_raw.reshape(b, s, SSD_DT_PAD)[:, :, :SSD_DT_COLS], (0, 2, 1))
    col_arr, row_arr = _dt_prep(dt_t, p["ssd_dt_bias"], p["ssd_a_log"])
    a_col = jnp.transpose(_group_dt_layout(col_arr, b, s), (0, 1, 3, 2))
    a_row = _group_dt_layout(row_arr, b, s)
    mix = _ssd(u_main, a_col, a_row, p["ssd_conv_w"], p["ssd_conv_b"], p["ssd_d_e"], p["ssd_norm"])
    x = layer_tail(0, mix, u_main, p["ssd_w_out"], x, "ssd_layer_tail")

    u, v_t = _norm_proj(x.reshape(tokens, d), p["norm_pre_mix"][1], p["diff_w_in"],
                        [(p["diff_w_in"].shape[1], BF16)], name="diff_in_proj", w_t=p["diff_w_v_t"])
    u = u.reshape(b, s, -1)
    mix = _diff_attn(u, v_t, p["rel_bias_table"], p["bias_tiles"], p["diff_lambda"], p["diff_subln"],
                     _lambda_init(1))
    return layer_tail(1, mix, u, p["diff_w_out"], x, "diff_layer_tail")


def kernel(x_prompt, x_sample, mem_prompt, mem_sample, rel_bias_table, norm_pre_mix, norm_post_mix,
           norm_pre_mlp, norm_post_mlp, ssd_w_in, ssd_conv_w, ssd_conv_b, ssd_dt_bias, ssd_a_log, ssd_d,
           ssd_norm, ssd_w_out, diff_w_in, diff_lambda, diff_subln, diff_w_out, x_mem_norm, x_w_kv,
           mlp_w1, mlp_w2):
    w_in = ssd_w_in[0]
    mix_cols = SSD_D_INNER + SSD_XBC
    w_in = jnp.concatenate([
        w_in[:, :mix_cols], w_in[:, mix_cols + SSD_DT_COLS:], w_in[:, mix_cols:mix_cols + SSD_DT_COLS],
        jnp.zeros((D_MODEL, SSD_DT_PAD - SSD_DT_COLS), w_in.dtype)], axis=1)
    p = {
        "rel_bias_table": rel_bias_table,
        "bias_tiles": _bias_tiles(rel_bias_table),
        "norm_pre_mix": norm_pre_mix, "norm_post_mix": norm_post_mix,
        "norm_pre_mlp": norm_pre_mlp, "norm_post_mlp": norm_post_mlp,
        "ssd_w_in": w_in.astype(BF16),
        "ssd_conv_w": ssd_conv_w[0], "ssd_conv_b": ssd_conv_b[0].reshape(1, SSD_XBC),
        "ssd_dt_bias": ssd_dt_bias[0], "ssd_a_log": ssd_a_log[0],
        "ssd_d_e": jnp.repeat(ssd_d[0], SSD_HEAD_DIM).reshape(1, SSD_D_INNER),
        "ssd_norm": ssd_norm[0].reshape(1, SSD_D_INNER),
        "ssd_w_out": ssd_w_out[0].astype(BF16),
        "diff_w_in": jnp.concatenate([
            diff_w_in[0][:, :DIFF_QK_COLS] * (DIFF_HEAD_DIM ** -0.5 * LOG2E),
            diff_w_in[0][:, DIFF_QK_COLS:2 * DIFF_QK_COLS],
            diff_w_in[0][:, 2 * DIFF_QK_COLS + DIFF_WIDTH:]], axis=1).astype(BF16),
        "diff_w_v_t": diff_w_in[0][:, 2 * DIFF_QK_COLS:2 * DIFF_QK_COLS + DIFF_WIDTH].T.astype(BF16),
        "diff_lambda": diff_lambda[0], "diff_subln": diff_subln[0],
        "diff_w_out": diff_w_out[0].astype(BF16),
        "x_mem_norm": x_mem_norm, "x_w_kv": x_w_kv.astype(BF16),
        "mlp_w1": mlp_w1.astype(BF16), "mlp_w2": mlp_w2.astype(BF16),
    }
    return (_trunk(x_prompt, mem_prompt, p), _trunk(x_sample, mem_sample, p))
```
